```python
import math
import jax, jax.numpy as jnp
from jax import lax
import numpy as np

D_MODEL = 1024
BATCH = 2
SEQ = 8192
DEPTH = 1

CHUNK = 64
POOL_WINDOWS = (2, 4, 8, 16)
N_POOL_GROUPS = len(POOL_WINDOWS)
D_POOL = D_MODEL
POOL_GROUP = D_POOL // N_POOL_GROUPS
RET_HEADS = 4
RET_DK = D_MODEL // RET_HEADS
RET_DV = 2 * D_MODEL // RET_HEADS
D_RET_QK = RET_HEADS * RET_DK
D_RET_V = RET_HEADS * RET_DV
ROPE_BASE = 10000.0
NORM_EPS = 1e-6
IN_SPLITS = (D_POOL, D_POOL, D_RET_QK, D_RET_QK, D_RET_V, D_RET_V, D_MODEL, D_MODEL)
D_IN = sum(IN_SPLITS)

kernel_name = "hybrid_pool_retention_gated_block"


def rms_norm(x, gain, eps=NORM_EPS):
    xf = x.astype(jnp.float32)
    y = xf * lax.rsqrt(jnp.mean(xf * xf, axis=-1, keepdims=True) + eps)
    return (y * gain.astype(jnp.float32)).astype(x.dtype)


def rotary(t, pos):
    half = t.shape[-1] // 2
    inv_freq = 1.0 / (ROPE_BASE ** (jnp.arange(half, dtype=jnp.float32) / half))
    ang = pos.astype(jnp.float32)[:, None] * inv_freq[None, :]
    cos = jnp.cos(ang)[None, :, None, :]
    sin = jnp.sin(ang)[None, :, None, :]
    t1, t2 = t[..., :half], t[..., half:]
    return jnp.concatenate([t1 * cos - t2 * sin, t1 * sin + t2 * cos], axis=-1)


def pool_mixer(u, w_pool, pool_scale):
    b, s, _ = u.shape
    uf = u.astype(jnp.float32)
    pos1 = jnp.arange(s, dtype=jnp.int32) + 1
    outs = []
    for g, w in enumerate(POOL_WINDOWS):
        seg = uf[..., g * POOL_GROUP:(g + 1) * POOL_GROUP]
        cs = jnp.cumsum(seg, axis=1)
        prev = jnp.pad(cs, ((0, 0), (w, 0), (0, 0)))[:, :s]
        cnt = jnp.minimum(pos1, w).astype(jnp.float32)[None, :, None]
        outs.append((cs - prev) / cnt - seg)
    pooled = jnp.stack(outs, axis=2)
    mixed = jnp.einsum('bsgc,gcd->bsgd', pooled, w_pool.astype(jnp.float32))
    mixed = mixed.reshape(b, s, D_POOL) * pool_scale.astype(jnp.float32)
    return mixed


def retention(q, k, v):
    b, s, h, dk = q.shape
    dv = v.shape[-1]
    nc = s // CHUNK
    gamma = 1.0 - jnp.exp2(-5.0 - jnp.arange(h, dtype=jnp.float32))
    log_g = jnp.log(gamma)
    idx = jnp.arange(CHUNK, dtype=jnp.float32)
    dist = jnp.abs(idx[:, None] - idx[None, :])
    decay_in = jnp.exp(log_g[:, None, None] * dist)
    xi = jnp.exp(log_g[None, :] * (idx[:, None] + 1.0))
    zeta = jnp.exp(log_g[None, :] * (CHUNK - 1.0 - idx[:, None]))
    g_chunk = jnp.exp(log_g * CHUNK)

    q = q * (dk ** -0.5)
    qc = q.reshape(b, nc, CHUNK, h, dk)
    kc = k.reshape(b, nc, CHUNK, h, dk)
    vc = v.reshape(b, nc, CHUNK, h, dv)

    scores = jnp.einsum('bnihd,bnjhd->bnhij', qc, kc) * decay_in[None, None]
    o_inner = jnp.einsum('bnhij,bnjhe->bnihe', scores, vc)

    def step(state, inp):
        q_i, k_i, v_i = inp
        cross = jnp.einsum('bihd,bhde->bihe', q_i, state) * xi[None, :, :, None]
        new_state = state * g_chunk[None, :, None, None] + jnp.einsum(
            'bihd,bihe->bhde', k_i * zeta[None, :, :, None], v_i)
        return new_state, cross

    state0 = jnp.zeros((b, h, dk, dv), jnp.float32)
    _, cross = lax.scan(step, state0, (jnp.moveaxis(qc, 1, 0), jnp.moveaxis(kc, 1, 0),
                                       jnp.moveaxis(vc, 1, 0)))
    o = o_inner + jnp.moveaxis(cross, 0, 1)
    o = o.reshape(b, s, h, dv)
    o = o * lax.rsqrt(jnp.mean(o * o, axis=-1, keepdims=True) + NORM_EPS)
    return o.reshape(b, s, h * dv)


def setup_inputs(seed: int = 0) -> dict:
    key = jax.random.key(seed)
    ks = jax.random.split(key, 10)
    f32 = jnp.float32
    x = jax.random.normal(ks[0], (BATCH, SEQ, D_MODEL), f32)
    norm_in = 1.0 + 0.02 * jax.random.normal(ks[1], (D_MODEL,), f32)
    w_in = jax.random.normal(ks[2], (D_MODEL, D_IN), f32) * D_MODEL ** -0.5
    w_pool = jax.random.normal(ks[3], (N_POOL_GROUPS, POOL_GROUP, POOL_GROUP), f32) * POOL_GROUP ** -0.5
    pool_scale = 1.0 + 0.02 * jax.random.normal(ks[4], (D_POOL,), f32)
    w_a = jax.random.normal(ks[5], (D_POOL, D_MODEL), f32) * D_POOL ** -0.5
    w_b = jax.random.normal(ks[6], (D_RET_V, D_MODEL), f32) * D_RET_V ** -0.5
    w_o = jax.random.normal(ks[7], (D_MODEL, D_MODEL), f32) * D_MODEL ** -0.5
    norm_out = 1.0 + 0.02 * jax.random.normal(ks[8], (D_MODEL,), f32)
    return {"x": x, "norm_in": norm_in, "w_in": w_in, "w_pool": w_pool,
            "pool_scale": pool_scale, "w_a": w_a, "w_b": w_b, "w_o": w_o,
            "norm_out": norm_out}


def reference(x, norm_in, w_in, w_pool, pool_scale, w_a, w_b, w_o, norm_out):
    b, s, _ = x.shape
    pos = jnp.arange(s, dtype=jnp.int32)
    for _layer in range(DEPTH):
        h = rms_norm(x, norm_in)
        proj = jnp.einsum('bsd,de->bse', h, w_in)
        offs = np.cumsum((0,) + IN_SPLITS)
        a_in, a_gate, q, k, v, r_gate, g_a, g_b = [
            proj[..., offs[i]:offs[i + 1]] for i in range(len(IN_SPLITS))]

        ya = pool_mixer(a_in, w_pool, pool_scale) * jax.nn.silu(a_gate.astype(jnp.float32))
        ya = jnp.einsum('bsc,cd->bsd', ya.astype(x.dtype), w_a)

        qf = rotary(q.astype(jnp.float32).reshape(b, s, RET_HEADS, RET_DK), pos)
        kf = rotary(k.astype(jnp.float32).reshape(b, s, RET_HEADS, RET_DK), pos)
        vf = v.astype(jnp.float32).reshape(b, s, RET_HEADS, RET_DV)
        yb = retention(qf, kf, vf) * jax.nn.silu(r_gate.astype(jnp.float32))
        yb = jnp.einsum('bsc,cd->bsd', yb.astype(x.dtype), w_b)

        merged = jax.nn.sigmoid(g_a) * ya + jax.nn.sigmoid(g_b) * yb
        x = x + jnp.einsum('bsd,de->bse', merged, w_o)
    return rms_norm(x, norm_out)
```

```python
import functools
import math

import jax
import jax.numpy as jnp
import numpy as np
from jax import lax
from jax.experimental import pallas as pl
from jax.experimental.pallas import tpu as pltpu

D_MODEL = 1024
CHUNK = 64
POOL_WINDOWS = (2, 4, 8, 16)
POOL_GROUP = D_MODEL // len(POOL_WINDOWS)
POOL_HALO = max(POOL_WINDOWS)
HEADS = 4
DK = D_MODEL // HEADS
DV = 2 * D_MODEL // HEADS
HALF = DK // 2
ROPE_BASE = 10000.0
EPS = 1e-6

OFF_A_IN, OFF_A_GATE, OFF_Q, OFF_K, OFF_V, OFF_RGATE, OFF_GA, OFF_GB, D_IN = (
    0, 1024, 2048, 3072, 4096, 6144, 8192, 9216, 10240)

TILE = 256
VMEM_LIMIT_BYTES = 58 * 1024 * 1024

F32 = jnp.float32
BF16 = jnp.bfloat16


def _gammas():
    return [1.0 - 2.0 ** (-5.0 - h) for h in range(HEADS)]


def _silu(v):
    return v * jax.nn.sigmoid(v)


def _block_kernel(x_ref, cos_ref, sin_ref, nin_ref, nout_ref, pscale_ref,
                  dmask_ref, xi_ref, zeta_ref,
                  w_in_ref, w_pool_ref, w_a_ref, w_b_ref, w_o_ref,
                  o_ref,
                  ubuf, state, ybin):
    t = pl.program_id(1)

    @pl.when(t == 0)
    def _reset_carries():
        ubuf[0:POOL_HALO, :] = jnp.zeros((POOL_HALO, D_MODEL), F32)
        state[...] = jnp.zeros(state.shape, F32)

    x = x_ref[...]
    h = x * lax.rsqrt(jnp.mean(x * x, axis=-1, keepdims=True) + EPS) * nin_ref[...]
    hb = h.astype(BF16)

    def proj(lo, hi):
        return jnp.dot(hb, w_in_ref[:, lo:hi], preferred_element_type=F32)

    ubuf[POOL_HALO:POOL_HALO + TILE, :] = proj(OFF_A_IN, OFF_A_GATE)
    a_gate = proj(OFF_A_GATE, OFF_Q)
    pos1 = lax.broadcasted_iota(jnp.int32, (TILE, POOL_GROUP), 0) + (t * TILE + 1)
    mixed = []
    for g, w in enumerate(POOL_WINDOWS):
        cols = slice(g * POOL_GROUP, (g + 1) * POOL_GROUP)
        u_g = ubuf[POOL_HALO:POOL_HALO + TILE, cols]
        win = u_g
        for k in range(1, w):
            win = win + ubuf[POOL_HALO - k:POOL_HALO - k + TILE, cols]
        cnt = jnp.minimum(pos1, w).astype(F32)
        pooled = win / cnt - u_g
        mixed.append(jnp.dot(pooled.astype(BF16), w_pool_ref[g], preferred_element_type=F32))
    ubuf[0:POOL_HALO, :] = ubuf[TILE:TILE + POOL_HALO, :]
    ya_in = jnp.concatenate(mixed, axis=1) * pscale_ref[...] * _silu(a_gate)
    ya = jnp.dot(ya_in.astype(BF16), w_a_ref[...], preferred_element_type=F32)
    merged = jax.nn.sigmoid(proj(OFF_GA, OFF_GB)) * ya

    q_all = proj(OFF_Q, OFF_K)
    k_all = proj(OFF_K, OFF_V)
    v_all = proj(OFF_V, OFF_RGATE)
    rg_all = proj(OFF_RGATE, OFF_GA)
    cos = cos_ref[...]
    sin = sin_ref[...]

    def rotary(m, hd):
        m1 = m[:, hd * DK:hd * DK + HALF]
        m2 = m[:, hd * DK + HALF:(hd + 1) * DK]
        return jnp.concatenate([m1 * cos - m2 * sin, m1 * sin + m2 * cos], axis=1)

    gam = _gammas()
    for hd in range(HEADS):
        qb = (rotary(q_all, hd) * (DK ** -0.5)).astype(BF16)
        kr = rotary(k_all, hd)
        kb = kr.astype(BF16)
        vb = v_all[:, hd * DV:(hd + 1) * DV].astype(BF16)
        scores = lax.dot_general(qb, kb, (((1,), (1,)), ((), ())), preferred_element_type=F32)
        p = (scores * dmask_ref[hd]).astype(BF16)
        o = jnp.dot(p, vb, preferred_element_type=F32)
        st = state[hd]
        cross = jnp.dot(qb, st.astype(BF16), preferred_element_type=F32)
        xi = xi_ref[hd]
        o = o + cross * jnp.concatenate([xi] * (DV // 128), axis=1)
        zeta = zeta_ref[hd]
        kz = (kr * jnp.concatenate([zeta] * (DK // 128), axis=1)).astype(BF16)
        upd = lax.dot_general(kz, vb, (((0,), (0,)), ((), ())), preferred_element_type=F32)
        state[hd] = st * (gam[hd] ** TILE) + upd
        on = o * lax.rsqrt(jnp.mean(o * o, axis=-1, keepdims=True) + EPS)
        rg = rg_all[:, hd * DV:(hd + 1) * DV]
        ybin[:, hd * DV:(hd + 1) * DV] = (on * _silu(rg)).astype(BF16)
    yb = jnp.dot(ybin[...], w_b_ref[...], preferred_element_type=F32)

    merged = merged + jax.nn.sigmoid(proj(OFF_GB, D_IN)) * yb
    y = x + jnp.dot(merged.astype(BF16), w_o_ref[...], preferred_element_type=F32)
    o_ref[...] = y * lax.rsqrt(jnp.mean(y * y, axis=-1, keepdims=True) + EPS) * nout_ref[...]


def _decay_tables():
    log_g = jnp.log(jnp.asarray(_gammas(), F32))
    idx = jnp.arange(TILE, dtype=jnp.int32)
    diff = idx[:, None] - idx[None, :]
    same = (idx[:, None] // CHUNK) == (idx[None, :] // CHUNK)
    dist = jnp.where(same, jnp.abs(diff), diff).astype(F32)
    keep = same | (diff > 0)
    dmask = jnp.where(keep[None], jnp.exp(log_g[:, None, None] * dist[None]), 0.0)
    fi = idx.astype(F32)
    xi = jnp.exp(log_g[:, None] * (fi[None, :] + 1.0))
    zeta = jnp.exp(log_g[:, None] * (TILE - 1.0 - fi[None, :]))
    rep = lambda a: jnp.broadcast_to(a[:, :, None], (HEADS, TILE, 128))
    return dmask.astype(F32), rep(xi), rep(zeta)


def _rope_tables(seq):
    inv_freq = 1.0 / (ROPE_BASE ** (jnp.arange(HALF, dtype=F32) / HALF))
    ang = jnp.arange(seq, dtype=jnp.int32).astype(F32)[:, None] * inv_freq[None, :]
    return jnp.cos(ang), jnp.sin(ang)


def _resident(shape):
    zeros = (0,) * len(shape)
    return pl.BlockSpec(shape, lambda b, t: zeros, pipeline_mode=pl.Buffered(1))


@jax.jit
def kernel(x, norm_in, w_in, w_pool, pool_scale, w_a, w_b, w_o, norm_out):
    batch, seq, d = x.shape
    assert d == D_MODEL and seq % TILE == 0 and TILE % CHUNK == 0
    nt = seq // TILE
    cos, sin = _rope_tables(seq)
    dmask, xi, zeta = _decay_tables()
    row = lambda v: v.reshape(1, -1).astype(F32)
    tok_spec = pl.BlockSpec((TILE, D_MODEL), lambda b, t: (b * nt + t, 0))
    rope_spec = pl.BlockSpec((TILE, HALF), lambda b, t: (t, 0))
    out = pl.pallas_call(
        _block_kernel,
        grid=(batch, nt),
        in_specs=[
            tok_spec, rope_spec, rope_spec,
            _resident((1, D_MODEL)), _resident((1, D_MODEL)), _resident((1, D_MODEL)),
            _resident(dmask.shape), _resident(xi.shape), _resident(zeta.shape),
            _resident((D_MODEL, D_IN)), _resident(w_pool.shape), _resident(w_a.shape),
            _resident(w_b.shape), _resident(w_o.shape),
        ],
        out_specs=tok_spec,
        out_shape=jax.ShapeDtypeStruct((batch * seq, D_MODEL), x.dtype),
        scratch_shapes=[
            pltpu.VMEM((POOL_HALO + TILE, D_MODEL), F32),
            pltpu.VMEM((HEADS, DK, DV), F32),
            pltpu.VMEM((TILE, HEADS * DV), BF16),
        ],
        compiler_params=pltpu.CompilerParams(
            dimension_semantics=("arbitrary", "arbitrary"),
            vmem_limit_bytes=VMEM_LIMIT_BYTES),
        name="hybrid_block",
    )(x.reshape(batch * seq, D_MODEL), cos, sin,
      row(norm_in), row(norm_out), row(pool_scale), dmask, xi, zeta,
      w_in.astype(BF16), w_pool.astype(BF16), w_a.astype(BF16),
      w_b.astype(BF16), w_o.astype(BF16))
    return out.reshape(batch, seq, D_MODEL)
```

```python
import jax
import jax.numpy as jnp
import numpy as np
from jax import lax
from jax.experimental import pallas as pl
from jax.experimental.pallas import tpu as pltpu

D_MODEL = 1024
CHUNK = 64
POOL_WINDOWS = (2, 4, 8, 16)
POOL_GROUP = D_MODEL // len(POOL_WINDOWS)
POOL_HALO = max(POOL_WINDOWS)
HEADS = 4
DK = D_MODEL // HEADS
DV = 2 * D_MODEL // HEADS
HALF = DK // 2
LANES = 128
ROPE_BASE = 10000.0
EPS = 1e-6

OFF_A_IN, OFF_A_GATE, OFF_Q, OFF_K, OFF_V, OFF_RGATE, OFF_GA, OFF_GB, D_IN = (
    0, 1024, 2048, 3072, 4096, 6144, 8192, 9216, 10240)

TILE = 256
VMEM_LIMIT_BYTES = 58 * 1024 * 1024

F32 = jnp.float32
BF16 = jnp.bfloat16


def _gammas():
    return [1.0 - 2.0 ** (-5.0 - h) for h in range(HEADS)]


def _silu(v):
    return v * jax.nn.sigmoid(v)


def _block_kernel(x_ref, cosb_ref, sinb_ref, cosr_ref, sinr_ref,
                  nin_ref, nout_ref, pscale_ref, dmask_ref, xi_ref, zeta_ref,
                  w_in_ref, w_pool_ref, w_a_ref, w_b_ref, w_o_ref,
                  o_ref,
                  ubuf, state, ybin):
    t = pl.program_id(1)

    @pl.when(t == 0)
    def _reset_carries():
        ubuf[0:POOL_HALO, :] = jnp.zeros((POOL_HALO, D_MODEL), F32)
        state[...] = jnp.zeros(state.shape, F32)

    x = x_ref[...]
    h = x * lax.rsqrt(jnp.mean(x * x, axis=-1, keepdims=True) + EPS) * nin_ref[...]
    hb = h.astype(BF16)

    def proj(lo, hi):
        return jnp.dot(hb, w_in_ref[:, lo:hi], preferred_element_type=F32)

    ubuf[POOL_HALO:POOL_HALO + TILE, :] = proj(OFF_A_IN, OFF_A_GATE)
    a_gate = proj(OFF_A_GATE, OFF_Q)
    pos1 = lax.broadcasted_iota(jnp.int32, (TILE, POOL_GROUP), 0) + (t * TILE + 1)
    mixed = []
    for g, w in enumerate(POOL_WINDOWS):
        cols = slice(g * POOL_GROUP, (g + 1) * POOL_GROUP)
        u_g = ubuf[POOL_HALO:POOL_HALO + TILE, cols]
        win = u_g
        for k in range(1, w):
            win = win + ubuf[POOL_HALO - k:POOL_HALO - k + TILE, cols]
        cnt = jnp.minimum(pos1, w).astype(F32)
        pooled = win / cnt - u_g
        mixed.append(jnp.dot(pooled.astype(BF16), w_pool_ref[g], preferred_element_type=F32))
    ubuf[0:POOL_HALO, :] = ubuf[TILE:TILE + POOL_HALO, :]
    ya_in = jnp.concatenate(mixed, axis=1) * pscale_ref[...] * _silu(a_gate)
    ya = jnp.dot(ya_in.astype(BF16), w_a_ref[...], preferred_element_type=F32)
    merged = jax.nn.sigmoid(proj(OFF_GA, OFF_GB)) * ya

    q_all = proj(OFF_Q, OFF_K)
    k_all = proj(OFF_K, OFF_V)
    v_all = proj(OFF_V, OFF_RGATE)
    rg_all = proj(OFF_RGATE, OFF_GA)
    cb, sb = cosb_ref[0], sinb_ref[0]
    cr, sr = cosr_ref[...], sinr_ref[...]
    cos = cb * cr - sb * sr
    sin = sb * cr + cb * sr

    def rotary(m, hd):
        m1 = m[:, hd * DK:hd * DK + HALF]
        m2 = m[:, hd * DK + HALF:(hd + 1) * DK]
        return jnp.concatenate([m1 * cos - m2 * sin, m1 * sin + m2 * cos], axis=1)

    gam = _gammas()
    for hd in range(HEADS):
        qb = (rotary(q_all, hd) * (DK ** -0.5)).astype(BF16)
        kr = rotary(k_all, hd)
        kb = kr.astype(BF16)
        vb = v_all[:, hd * DV:(hd + 1) * DV].astype(BF16)
        scores = lax.dot_general(qb, kb, (((1,), (1,)), ((), ())), preferred_element_type=F32)
        p = (scores * dmask_ref[hd]).astype(BF16)
        o = jnp.dot(p, vb, preferred_element_type=F32)
        st = state[hd]
        cross = jnp.dot(qb, st.astype(BF16), preferred_element_type=F32)
        xi = xi_ref[hd]
        o = o + cross * jnp.concatenate([xi] * (DV // LANES), axis=1)
        zeta = zeta_ref[hd]
        kz = (kr * jnp.concatenate([zeta] * (DK // LANES), axis=1)).astype(BF16)
        upd = lax.dot_general(kz, vb, (((0,), (0,)), ((), ())), preferred_element_type=F32)
        state[hd] = st * (gam[hd] ** TILE) + upd
        on = o * lax.rsqrt(jnp.mean(o * o, axis=-1, keepdims=True) + EPS)
        rg = rg_all[:, hd * DV:(hd + 1) * DV]
        ybin[:, hd * DV:(hd + 1) * DV] = (on * _silu(rg)).astype(BF16)
    yb = jnp.dot(ybin[...], w_b_ref[...], preferred_element_type=F32)

    merged = merged + jax.nn.sigmoid(proj(OFF_GB, D_IN)) * yb
    y = x + jnp.dot(merged.astype(BF16), w_o_ref[...], preferred_element_type=F32)
    o_ref[...] = y * lax.rsqrt(jnp.mean(y * y, axis=-1, keepdims=True) + EPS) * nout_ref[...]


def _decay_tables():
    log_g = np.log(np.asarray(_gammas(), np.float64))
    idx = np.arange(TILE)
    diff = idx[:, None] - idx[None, :]
    same = (idx[:, None] // CHUNK) == (idx[None, :] // CHUNK)
    dist = np.where(same, np.abs(diff), diff).astype(np.float64)
    keep = same | (diff > 0)
    dmask = np.where(keep[None], np.exp(log_g[:, None, None] * dist[None]), 0.0)
    xi = np.exp(log_g[:, None] * (idx[None, :] + 1.0))
    zeta = np.exp(log_g[:, None] * (TILE - 1.0 - idx[None, :]))
    rep = lambda a: np.ascontiguousarray(
        np.broadcast_to(a[:, :, None], (HEADS, TILE, LANES)), dtype=np.float32)
    return dmask.astype(np.float32), rep(xi), rep(zeta)


def _rope_tables(nt):
    inv_freq = 1.0 / (ROPE_BASE ** (jnp.arange(HALF, dtype=F32) / HALF))
    base = (jnp.arange(nt, dtype=jnp.int32) * TILE).astype(F32)[:, None] * inv_freq[None, :]
    rel = jnp.arange(TILE, dtype=jnp.int32).astype(F32)[:, None] * inv_freq[None, :]
    base = base.reshape(nt, 1, HALF)
    return jnp.cos(base), jnp.sin(base), jnp.cos(rel), jnp.sin(rel)


def _resident(shape):
    zeros = (0,) * len(shape)
    return pl.BlockSpec(shape, lambda b, t: zeros, pipeline_mode=pl.Buffered(1))


@jax.jit
def kernel(x, norm_in, w_in, w_pool, pool_scale, w_a, w_b, w_o, norm_out):
    batch, seq, d = x.shape
    assert d == D_MODEL and seq % TILE == 0 and TILE % CHUNK == 0
    nt = seq // TILE
    cosb, sinb, cosr, sinr = _rope_tables(nt)
    dmask, xi, zeta = _decay_tables()
    row = lambda v: v.reshape(1, -1).astype(F32)
    tok_spec = pl.BlockSpec((TILE, D_MODEL), lambda b, t: (b * nt + t, 0))
    base_spec = pl.BlockSpec((1, 1, HALF), lambda b, t: (t, 0, 0))
    out = pl.pallas_call(
        _block_kernel,
        grid=(batch, nt),
        in_specs=[
            tok_spec, base_spec, base_spec, _resident((TILE, HALF)), _resident((TILE, HALF)),
            _resident((1, D_MODEL)), _resident((1, D_MODEL)), _resident((1, D_MODEL)),
            _resident(dmask.shape), _resident(xi.shape), _resident(zeta.shape),
            _resident((D_MODEL, D_IN)), _resident(w_pool.shape), _resident(w_a.shape),
            _resident(w_b.shape), _resident(w_o.shape),
        ],
        out_specs=tok_spec,
        out_shape=jax.ShapeDtypeStruct((batch * seq, D_MODEL), x.dtype),
        scratch_shapes=[
            pltpu.VMEM((POOL_HALO + TILE, D_MODEL), F32),
            pltpu.VMEM((HEADS, DK, DV), F32),
            pltpu.VMEM((TILE, HEADS * DV), BF16),
        ],
        compiler_params=pltpu.CompilerParams(
            dimension_semantics=("arbitrary", "arbitrary"),
            vmem_limit_bytes=VMEM_LIMIT_BYTES),
        name="hybrid_block",
    )(x.reshape(batch * seq, D_MODEL), cosb, sinb, cosr, sinr,
      row(norm_in), row(norm_out), row(pool_scale), dmask, xi, zeta,
      w_in.astype(BF16), w_pool.astype(BF16), w_a.astype(BF16),
      w_b.astype(BF16), w_o.astype(BF16))
    return out.reshape(batch, seq, D_MODEL)
```

```python
import jax
import jax.numpy as jnp
import numpy as np
from jax import lax
from jax.experimental import pallas as pl
from jax.experimental.pallas import tpu as pltpu

D_MODEL = 1024
CHUNK = 64
POOL_WINDOWS = (2, 4, 8, 16)
POOL_GROUP = D_MODEL // len(POOL_WINDOWS)
POOL_HALO = max(POOL_WINDOWS)
HEADS = 4
DK = D_MODEL // HEADS
DV = 2 * D_MODEL // HEADS
HALF = DK // 2
LANES = 128
ROPE_BASE = 10000.0
EPS = 1e-6

OFF_A_IN, OFF_A_GATE, OFF_Q, OFF_K, OFF_V, OFF_RGATE, OFF_GA, OFF_GB, D_IN = (
    0, 1024, 2048, 3072, 4096, 6144, 8192, 9216, 10240)

TILE = 512
RBLK = 256
VMEM_LIMIT_BYTES = 58 * 1024 * 1024

F32 = jnp.float32
BF16 = jnp.bfloat16


def _gammas():
    return [1.0 - 2.0 ** (-5.0 - h) for h in range(HEADS)]


def _silu(v):
    return v * jax.nn.sigmoid(v)


def _block_kernel(x_ref, cosb_ref, sinb_ref, cosr_ref, sinr_ref,
                  nin_ref, nout_ref, pscale_ref, dmask_ref, xi_ref, zeta_ref,
                  w_in_ref, w_pool_ref, w_a_ref, w_b_ref, w_o_ref,
                  o_ref,
                  ubuf, state, ybin):
    t = pl.program_id(1)

    @pl.when(t == 0)
    def _reset_carries():
        ubuf[0:POOL_HALO, :] = jnp.zeros((POOL_HALO, D_MODEL), F32)
        state[...] = jnp.zeros(state.shape, F32)

    x = x_ref[...]
    h = x * lax.rsqrt(jnp.mean(x * x, axis=-1, keepdims=True) + EPS) * nin_ref[...]
    hb = h.astype(BF16)

    def proj(lo, hi):
        return jnp.dot(hb, w_in_ref[:, lo:hi], preferred_element_type=F32)

    ubuf[POOL_HALO:POOL_HALO + TILE, :] = proj(OFF_A_IN, OFF_A_GATE)
    a_gate = proj(OFF_A_GATE, OFF_Q)
    pos1 = lax.broadcasted_iota(jnp.int32, (TILE, POOL_GROUP), 0) + (t * TILE + 1)
    mixed = []
    for g, w in enumerate(POOL_WINDOWS):
        cols = slice(g * POOL_GROUP, (g + 1) * POOL_GROUP)
        u_g = ubuf[POOL_HALO:POOL_HALO + TILE, cols]
        win = u_g
        for k in range(1, w):
            win = win + ubuf[POOL_HALO - k:POOL_HALO - k + TILE, cols]
        cnt = jnp.minimum(pos1, w).astype(F32)
        pooled = win / cnt - u_g
        mixed.append(jnp.dot(pooled.astype(BF16), w_pool_ref[g], preferred_element_type=F32))
    ubuf[0:POOL_HALO, :] = ubuf[TILE:TILE + POOL_HALO, :]
    ya_in = jnp.concatenate(mixed, axis=1) * pscale_ref[...] * _silu(a_gate)
    ya = jnp.dot(ya_in.astype(BF16), w_a_ref[...], preferred_element_type=F32)
    merged = jax.nn.sigmoid(proj(OFF_GA, OFF_GB)) * ya

    q_all = proj(OFF_Q, OFF_K)
    k_all = proj(OFF_K, OFF_V)
    v_all = proj(OFF_V, OFF_RGATE)
    rg_all = proj(OFF_RGATE, OFF_GA)
    cb, sb = cosb_ref[0], sinb_ref[0]
    cr, sr = cosr_ref[...], sinr_ref[...]
    cos = cb * cr - sb * sr
    sin = sb * cr + cb * sr

    def rotary(m, hd, rows):
        m1 = m[rows, hd * DK:hd * DK + HALF]
        m2 = m[rows, hd * DK + HALF:(hd + 1) * DK]
        c, s = cos[rows], sin[rows]
        return jnp.concatenate([m1 * c - m2 * s, m1 * s + m2 * c], axis=1)

    gam = _gammas()
    for rb in range(TILE // RBLK):
        rows = slice(rb * RBLK, (rb + 1) * RBLK)
        for hd in range(HEADS):
            qb = (rotary(q_all, hd, rows) * (DK ** -0.5)).astype(BF16)
            kr = rotary(k_all, hd, rows)
            kb = kr.astype(BF16)
            vb = v_all[rows, hd * DV:(hd + 1) * DV].astype(BF16)
            scores = lax.dot_general(qb, kb, (((1,), (1,)), ((), ())), preferred_element_type=F32)
            p = (scores * dmask_ref[hd]).astype(BF16)
            o = jnp.dot(p, vb, preferred_element_type=F32)
            st = state[hd]
            cross = jnp.dot(qb, st.astype(BF16), preferred_element_type=F32)
            xi = xi_ref[hd]
            o = o + cross * jnp.concatenate([xi] * (DV // LANES), axis=1)
            zeta = zeta_ref[hd]
            kz = (kr * jnp.concatenate([zeta] * (DK // LANES), axis=1)).astype(BF16)
            upd = lax.dot_general(kz, vb, (((0,), (0,)), ((), ())), preferred_element_type=F32)
            state[hd] = st * (gam[hd] ** RBLK) + upd
            on = o * lax.rsqrt(jnp.mean(o * o, axis=-1, keepdims=True) + EPS)
            rg = rg_all[rows, hd * DV:(hd + 1) * DV]
            ybin[rows, hd * DV:(hd + 1) * DV] = (on * _silu(rg)).astype(BF16)
    yb = jnp.dot(ybin[...], w_b_ref[...], preferred_element_type=F32)

    merged = merged + jax.nn.sigmoid(proj(OFF_GB, D_IN)) * yb
    y = x + jnp.dot(merged.astype(BF16), w_o_ref[...], preferred_element_type=F32)
    o_ref[...] = y * lax.rsqrt(jnp.mean(y * y, axis=-1, keepdims=True) + EPS) * nout_ref[...]


def _decay_tables():
    log_g = np.log(np.asarray(_gammas(), np.float64))
    idx = np.arange(RBLK)
    diff = idx[:, None] - idx[None, :]
    same = (idx[:, None] // CHUNK) == (idx[None, :] // CHUNK)
    dist = np.where(same, np.abs(diff), diff).astype(np.float64)
    keep = same | (diff > 0)
    dmask = np.where(keep[None], np.exp(log_g[:, None, None] * dist[None]), 0.0)
    xi = np.exp(log_g[:, None] * (idx[None, :] + 1.0))
    zeta = np.exp(log_g[:, None] * (RBLK - 1.0 - idx[None, :]))
    rep = lambda a: np.ascontiguousarray(
        np.broadcast_to(a[:, :, None], (HEADS, RBLK, LANES)), dtype=np.float32)
    return dmask.astype(np.float32), rep(xi), rep(zeta)


def _rope_tables(nt):
    inv_freq = 1.0 / (ROPE_BASE ** (jnp.arange(HALF, dtype=F32) / HALF))
    base = (jnp.arange(nt, dtype=jnp.int32) * TILE).astype(F32)[:, None] * inv_freq[None, :]
    rel = jnp.arange(TILE, dtype=jnp.int32).astype(F32)[:, None] * inv_freq[None, :]
    base = base.reshape(nt, 1, HALF)
    return jnp.cos(base), jnp.sin(base), jnp.cos(rel), jnp.sin(rel)


def _resident(shape):
    zeros = (0,) * len(shape)
    return pl.BlockSpec(shape, lambda b, t: zeros, pipeline_mode=pl.Buffered(1))


@jax.jit
def kernel(x, norm_in, w_in, w_pool, pool_scale, w_a, w_b, w_o, norm_out):
    batch, seq, d = x.shape
    assert d == D_MODEL and seq % TILE == 0 and TILE % RBLK == 0 and RBLK % CHUNK == 0
    nt = seq // TILE
    cosb, sinb, cosr, sinr = _rope_tables(nt)
    dmask, xi, zeta = _decay_tables()
    row = lambda v: v.reshape(1, -1).astype(F32)
    tok_spec = pl.BlockSpec((TILE, D_MODEL), lambda b, t: (b * nt + t, 0))
    base_spec = pl.BlockSpec((1, 1, HALF), lambda b, t: (t, 0, 0))
    out = pl.pallas_call(
        _block_kernel,
        grid=(batch, nt),
        in_specs=[
            tok_spec, base_spec, base_spec, _resident((TILE, HALF)), _resident((TILE, HALF)),
            _resident((1, D_MODEL)), _resident((1, D_MODEL)), _resident((1, D_MODEL)),
            _resident(dmask.shape), _resident(xi.shape), _resident(zeta.shape),
            _resident((D_MODEL, D_IN)), _resident(w_pool.shape), _resident(w_a.shape),
            _resident(w_b.shape), _resident(w_o.shape),
        ],
        out_specs=tok_spec,
        out_shape=jax.ShapeDtypeStruct((batch * seq, D_MODEL), x.dtype),
        scratch_shapes=[
            pltpu.VMEM((POOL_HALO + TILE, D_MODEL), F32),
            pltpu.VMEM((HEADS, DK, DV), F32),
            pltpu.VMEM((TILE, HEADS * DV), BF16),
        ],
        compiler_params=pltpu.CompilerParams(
            dimension_semantics=("arbitrary", "arbitrary"),
            vmem_limit_bytes=VMEM_LIMIT_BYTES),
        name="hybrid_block",
    )(x.reshape(batch * seq, D_MODEL), cosb, sinb, cosr, sinr,
      row(norm_in), row(norm_out), row(pool_scale), dmask, xi, zeta,
      w_in.astype(BF16), w_pool.astype(BF16), w_a.astype(BF16),
      w_b.astype(BF16), w_o.astype(BF16))
    return out.reshape(batch, seq, D_MODEL)
```

```python
import jax
import jax.numpy as jnp
import numpy as np
from jax import lax
from jax.experimental import pallas as pl
from jax.experimental.pallas import tpu as pltpu

D_MODEL = 1024
CHUNK = 64
POOL_WINDOWS = (2, 4, 8, 16)
POOL_GROUP = D_MODEL // len(POOL_WINDOWS)
POOL_HALO = max(POOL_WINDOWS)
HEADS = 4
DK = D_MODEL // HEADS
DV = 2 * D_MODEL // HEADS
HALF = DK // 2
LANES = 128
ROPE_BASE = 10000.0
EPS = 1e-6

OFF_A_IN, OFF_A_GATE, OFF_Q, OFF_K, OFF_V, OFF_RGATE, OFF_GA, OFF_GB, D_IN = (
    0, 1024, 2048, 3072, 4096, 6144, 8192, 9216, 10240)

TILE = 256
VMEM_LIMIT_BYTES = 58 * 1024 * 1024

F32 = jnp.float32
BF16 = jnp.bfloat16


def _gammas():
    return [1.0 - 2.0 ** (-5.0 - h) for h in range(HEADS)]


def _silu(v):
    return v * jax.nn.sigmoid(v)


def _block_kernel(x_ref, cosb_ref, sinb_ref, cosr_ref, sinr_ref,
                  nin_ref, nout_ref, pscale_ref, dmask_ref, xi_ref, zeta_ref,
                  w_in_ref, w_pool_ref, w_a_ref, w_b_ref, w_o_ref,
                  o_ref,
                  ubuf, state, ybin):
    t = pl.program_id(1)

    @pl.when(t == 0)
    def _reset_carries():
        ubuf[0:POOL_HALO, :] = jnp.zeros((POOL_HALO, D_MODEL), F32)
        state[...] = jnp.zeros(state.shape, F32)

    x = x_ref[...]
    h = x * lax.rsqrt(jnp.mean(x * x, axis=-1, keepdims=True) + EPS) * nin_ref[...]
    hb = h.astype(BF16)

    def proj(lo, hi):
        return jnp.dot(hb, w_in_ref[:, lo:hi], preferred_element_type=F32)

    ubuf[POOL_HALO:POOL_HALO + TILE, :] = proj(OFF_A_IN, OFF_A_GATE)
    q_all = proj(OFF_Q, OFF_K)
    k_all = proj(OFF_K, OFF_V)
    v_all = proj(OFF_V, OFF_RGATE)
    a_gate = proj(OFF_A_GATE, OFF_Q)
    rg_all = proj(OFF_RGATE, OFF_GA)
    g_a = proj(OFF_GA, OFF_GB)

    pos1 = lax.broadcasted_iota(jnp.int32, (TILE, POOL_GROUP), 0) + (t * TILE + 1)
    mixed = []
    for g, w in enumerate(POOL_WINDOWS):
        cols = slice(g * POOL_GROUP, (g + 1) * POOL_GROUP)
        u_g = ubuf[POOL_HALO:POOL_HALO + TILE, cols]
        win = u_g
        for k in range(1, w):
            win = win + ubuf[POOL_HALO - k:POOL_HALO - k + TILE, cols]
        cnt = jnp.minimum(pos1, w).astype(F32)
        pooled = win / cnt - u_g
        mixed.append(jnp.dot(pooled.astype(BF16), w_pool_ref[g], preferred_element_type=F32))
    ubuf[0:POOL_HALO, :] = ubuf[TILE:TILE + POOL_HALO, :]
    ya_in = jnp.concatenate(mixed, axis=1) * pscale_ref[...] * _silu(a_gate)
    ya = jnp.dot(ya_in.astype(BF16), w_a_ref[...], preferred_element_type=F32)
    merged = jax.nn.sigmoid(g_a) * ya

    cb, sb = cosb_ref[0], sinb_ref[0]
    cr, sr = cosr_ref[...], sinr_ref[...]
    cos = cb * cr - sb * sr
    sin = sb * cr + cb * sr

    def rotary(m, hd):
        m1 = m[:, hd * DK:hd * DK + HALF]
        m2 = m[:, hd * DK + HALF:(hd + 1) * DK]
        return jnp.concatenate([m1 * cos - m2 * sin, m1 * sin + m2 * cos], axis=1)

    gam = _gammas()
    vbs, ps, crosses = [], [], []
    for hd in range(HEADS):
        qb = (rotary(q_all, hd) * (DK ** -0.5)).astype(BF16)
        kr = rotary(k_all, hd)
        vb = v_all[:, hd * DV:(hd + 1) * DV].astype(BF16)
        scores = lax.dot_general(qb, kr.astype(BF16), (((1,), (1,)), ((), ())),
                                 preferred_element_type=F32)
        st = state[hd]
        xi = xi_ref[hd]
        crosses.append(jnp.dot(qb, st.astype(BF16), preferred_element_type=F32)
                       * jnp.concatenate([xi] * (DV // LANES), axis=1))
        zeta = zeta_ref[hd]
        kz = (kr * jnp.concatenate([zeta] * (DK // LANES), axis=1)).astype(BF16)
        upd = lax.dot_general(kz, vb, (((0,), (0,)), ((), ())), preferred_element_type=F32)
        state[hd] = st * (gam[hd] ** TILE) + upd
        ps.append((scores * dmask_ref[hd]).astype(BF16))
        vbs.append(vb)
    for hd in range(HEADS):
        o = jnp.dot(ps[hd], vbs[hd], preferred_element_type=F32) + crosses[hd]
        on = o * lax.rsqrt(jnp.mean(o * o, axis=-1, keepdims=True) + EPS)
        rg = rg_all[:, hd * DV:(hd + 1) * DV]
        ybin[:, hd * DV:(hd + 1) * DV] = (on * _silu(rg)).astype(BF16)
    g_b = proj(OFF_GB, D_IN)
    yb = jnp.dot(ybin[...], w_b_ref[...], preferred_element_type=F32)

    merged = merged + jax.nn.sigmoid(g_b) * yb
    y = x + jnp.dot(merged.astype(BF16), w_o_ref[...], preferred_element_type=F32)
    o_ref[...] = y * lax.rsqrt(jnp.mean(y * y, axis=-1, keepdims=True) + EPS) * nout_ref[...]


def _decay_tables():
    log_g = np.log(np.asarray(_gammas(), np.float64))
    idx = np.arange(TILE)
    diff = idx[:, None] - idx[None, :]
    same = (idx[:, None] // CHUNK) == (idx[None, :] // CHUNK)
    dist = np.where(same, np.abs(diff), diff).astype(np.float64)
    keep = same | (diff > 0)
    dmask = np.where(keep[None], np.exp(log_g[:, None, None] * dist[None]), 0.0)
    xi = np.exp(log_g[:, None] * (idx[None, :] + 1.0))
    zeta = np.exp(log_g[:, None] * (TILE - 1.0 - idx[None, :]))
    rep = lambda a: np.ascontiguousarray(
        np.broadcast_to(a[:, :, None], (HEADS, TILE, LANES)), dtype=np.float32)
    return dmask.astype(np.float32), rep(xi), rep(zeta)


def _rope_tables(nt):
    inv_freq = 1.0 / (ROPE_BASE ** (jnp.arange(HALF, dtype=F32) / HALF))
    base = (jnp.arange(nt, dtype=jnp.int32) * TILE).astype(F32)[:, None] * inv_freq[None, :]
    rel = jnp.arange(TILE, dtype=jnp.int32).astype(F32)[:, None] * inv_freq[None, :]
    base = base.reshape(nt, 1, HALF)
    return jnp.cos(base), jnp.sin(base), jnp.cos(rel), jnp.sin(rel)


def _resident(shape):
    zeros = (0,) * len(shape)
    return pl.BlockSpec(shape, lambda b, t: zeros, pipeline_mode=pl.Buffered(1))


@jax.jit
def kernel(x, norm_in, w_in, w_pool, pool_scale, w_a, w_b, w_o, norm_out):
    batch, seq, d = x.shape
    assert d == D_MODEL and seq % TILE == 0 and TILE % CHUNK == 0
    nt = seq // TILE
    cosb, sinb, cosr, sinr = _rope_tables(nt)
    dmask, xi, zeta = _decay_tables()
    row = lambda v: v.reshape(1, -1).astype(F32)
    tok_spec = pl.BlockSpec((TILE, D_MODEL), lambda b, t: (b * nt + t, 0))
    base_spec = pl.BlockSpec((1, 1, HALF), lambda b, t: (t, 0, 0))
    out = pl.pallas_call(
        _block_kernel,
        grid=(batch, nt),
        in_specs=[
            tok_spec, base_spec, base_spec, _resident((TILE, HALF)), _resident((TILE, HALF)),
            _resident((1, D_MODEL)), _resident((1, D_MODEL)), _resident((1, D_MODEL)),
            _resident(dmask.shape), _resident(xi.shape), _resident(zeta.shape),
            _resident((D_MODEL, D_IN)), _resident(w_pool.shape), _resident(w_a.shape),
            _resident(w_b.shape), _resident(w_o.shape),
        ],
        out_specs=tok_spec,
        out_shape=jax.ShapeDtypeStruct((batch * seq, D_MODEL), x.dtype),
        scratch_shapes=[
            pltpu.VMEM((POOL_HALO + TILE, D_MODEL), F32),
            pltpu.VMEM((HEADS, DK, DV), F32),
            pltpu.VMEM((TILE, HEADS * DV), BF16),
        ],
        compiler_params=pltpu.CompilerParams(
            dimension_semantics=("arbitrary", "arbitrary"),
            vmem_limit_bytes=VMEM_LIMIT_BYTES),
        name="hybrid_block",
    )(x.reshape(batch * seq, D_MODEL), cosb, sinb, cosr, sinr,
      row(norm_in), row(norm_out), row(pool_scale), dmask, xi, zeta,
      w_in.astype(BF16), w_pool.astype(BF16), w_a.astype(BF16),
      w_b.astype(BF16), w_o.astype(BF16))
    return out.reshape(batch, seq, D_MODEL)
```

```python
import jax
import jax.numpy as jnp
import numpy as np
from jax import lax
from jax.experimental import pallas as pl
from jax.experimental.pallas import tpu as pltpu

D_MODEL = 1024
CHUNK = 64
POOL_WINDOWS = (2, 4, 8, 16)
POOL_GROUP = D_MODEL // len(POOL_WINDOWS)
POOL_HALO = max(POOL_WINDOWS)
HEADS = 4
DK = D_MODEL // HEADS
DV = 2 * D_MODEL // HEADS
HALF = DK // 2
LANES = 128
ROPE_BASE = 10000.0
EPS = 1e-6

OFF_A_IN, OFF_A_GATE, OFF_Q, OFF_K, OFF_V, OFF_RGATE, OFF_GA, OFF_GB, D_IN = (
    0, 1024, 2048, 3072, 4096, 6144, 8192, 9216, 10240)

TILE = 256
VMEM_LIMIT_BYTES = 58 * 1024 * 1024

F32 = jnp.float32
BF16 = jnp.bfloat16


def _gammas():
    return [1.0 - 2.0 ** (-5.0 - h) for h in range(HEADS)]


def _silu(v):
    return v * jax.nn.sigmoid(v)


def _pack_rows(w):
    *lead, k, n = w.shape
    pairs = jnp.swapaxes(w.astype(BF16).reshape(*lead, k // 2, 2, n), -1, -2)
    return lax.bitcast_convert_type(pairs, jnp.uint32)


def _unpack_rows(words):
    return pltpu.bitcast(words, BF16)


def _block_kernel(x_ref, cosb_ref, sinb_ref, cosr_ref, sinr_ref,
                  nin_ref, nout_ref, pscale_ref, dmask_ref, xi_ref, zeta_ref,
                  w_in_ref, w_pool_ref, w_a_ref, w_b_ref, w_o_ref,
                  o_ref,
                  ubuf, state, ybin):
    t = pl.program_id(1)

    @pl.when(t == 0)
    def _reset_carries():
        ubuf[0:POOL_HALO, :] = jnp.zeros((POOL_HALO, D_MODEL), F32)
        state[...] = jnp.zeros(state.shape, F32)

    x = x_ref[...]
    h = x * lax.rsqrt(jnp.mean(x * x, axis=-1, keepdims=True) + EPS) * nin_ref[...]
    hb = h.astype(BF16)

    def proj(lo, hi):
        return jnp.dot(hb, _unpack_rows(w_in_ref[:, lo:hi]), preferred_element_type=F32)

    ubuf[POOL_HALO:POOL_HALO + TILE, :] = proj(OFF_A_IN, OFF_A_GATE)
    q_all = proj(OFF_Q, OFF_K)
    k_all = proj(OFF_K, OFF_V)
    v_all = proj(OFF_V, OFF_RGATE)
    a_gate = proj(OFF_A_GATE, OFF_Q)
    rg_all = proj(OFF_RGATE, OFF_GA)
    g_a = proj(OFF_GA, OFF_GB)

    pos1 = lax.broadcasted_iota(jnp.int32, (TILE, POOL_GROUP), 0) + (t * TILE + 1)
    mixed = []
    for g, w in enumerate(POOL_WINDOWS):
        cols = slice(g * POOL_GROUP, (g + 1) * POOL_GROUP)
        ext = ubuf[0:POOL_HALO + TILE, cols]
        win = ext
        span = 1
        while span < w:
            win = win + pltpu.roll(win, span, axis=0)
            span *= 2
        win = win[POOL_HALO:, :]
        u_g = ext[POOL_HALO:, :]
        cnt = jnp.minimum(pos1, w).astype(F32)
        pooled = win / cnt - u_g
        mixed.append(jnp.dot(pooled.astype(BF16), _unpack_rows(w_pool_ref[g]),
                             preferred_element_type=F32))
    ubuf[0:POOL_HALO, :] = ubuf[TILE:TILE + POOL_HALO, :]
    ya_in = jnp.concatenate(mixed, axis=1) * pscale_ref[...] * _silu(a_gate)
    ya = jnp.dot(ya_in.astype(BF16), _unpack_rows(w_a_ref[...]), preferred_element_type=F32)
    merged = jax.nn.sigmoid(g_a) * ya

    cb, sb = cosb_ref[0], sinb_ref[0]
    cr, sr = cosr_ref[...], sinr_ref[...]
    cos = cb * cr - sb * sr
    sin = sb * cr + cb * sr

    def rotary(m, hd):
        m1 = m[:, hd * DK:hd * DK + HALF]
        m2 = m[:, hd * DK + HALF:(hd + 1) * DK]
        return jnp.concatenate([m1 * cos - m2 * sin, m1 * sin + m2 * cos], axis=1)

    gam = _gammas()
    vbs, ps, crosses = [], [], []
    for hd in range(HEADS):
        qb = (rotary(q_all, hd) * (DK ** -0.5)).astype(BF16)
        kr = rotary(k_all, hd)
        vb = v_all[:, hd * DV:(hd + 1) * DV].astype(BF16)
        scores = lax.dot_general(qb, kr.astype(BF16), (((1,), (1,)), ((), ())),
                                 preferred_element_type=F32)
        st = state[hd]
        xi = xi_ref[hd]
        crosses.append(jnp.dot(qb, st.astype(BF16), preferred_element_type=F32)
                       * jnp.concatenate([xi] * (DV // LANES), axis=1))
        zeta = zeta_ref[hd]
        kz = (kr * jnp.concatenate([zeta] * (DK // LANES), axis=1)).astype(BF16)
        upd = lax.dot_general(kz, vb, (((0,), (0,)), ((), ())), preferred_element_type=F32)
        state[hd] = st * (gam[hd] ** TILE) + upd
        ps.append((scores * dmask_ref[hd]).astype(BF16))
        vbs.append(vb)
    for hd in range(HEADS):
        o = jnp.dot(ps[hd], vbs[hd], preferred_element_type=F32) + crosses[hd]
        on = o * lax.rsqrt(jnp.mean(o * o, axis=-1, keepdims=True) + EPS)
        rg = rg_all[:, hd * DV:(hd + 1) * DV]
        ybin[:, hd * DV:(hd + 1) * DV] = (on * _silu(rg)).astype(BF16)
    g_b = proj(OFF_GB, D_IN)
    yb = jnp.dot(ybin[...], _unpack_rows(w_b_ref[...]), preferred_element_type=F32)

    merged = merged + jax.nn.sigmoid(g_b) * yb
    y = x + jnp.dot(merged.astype(BF16), _unpack_rows(w_o_ref[...]), preferred_element_type=F32)
    o_ref[...] = y * lax.rsqrt(jnp.mean(y * y, axis=-1, keepdims=True) + EPS) * nout_ref[...]


def _decay_tables():
    log_g = np.log(np.asarray(_gammas(), np.float64))
    idx = np.arange(TILE)
    diff = idx[:, None] - idx[None, :]
    same = (idx[:, None] // CHUNK) == (idx[None, :] // CHUNK)
    dist = np.where(same, np.abs(diff), diff).astype(np.float64)
    keep = same | (diff > 0)
    dmask = np.where(keep[None], np.exp(log_g[:, None, None] * dist[None]), 0.0)
    xi = np.exp(log_g[:, None] * (idx[None, :] + 1.0))
    zeta = np.exp(log_g[:, None] * (TILE - 1.0 - idx[None, :]))
    rep = lambda a: np.ascontiguousarray(
        np.broadcast_to(a[:, :, None], (HEADS, TILE, LANES)), dtype=np.float32)
    return dmask.astype(np.float32), rep(xi), rep(zeta)


def _rope_tables(nt):
    inv_freq = 1.0 / (ROPE_BASE ** (jnp.arange(HALF, dtype=F32) / HALF))
    base = (jnp.arange(nt, dtype=jnp.int32) * TILE).astype(F32)[:, None] * inv_freq[None, :]
    rel = jnp.arange(TILE, dtype=jnp.int32).astype(F32)[:, None] * inv_freq[None, :]
    base = base.reshape(nt, 1, HALF)
    return jnp.cos(base), jnp.sin(base), jnp.cos(rel), jnp.sin(rel)


def _packed_shape(w):
    return w.shape[:-2] + (w.shape[-2] // 2, w.shape[-1])


def _resident(shape):
    zeros = (0,) * len(shape)
    return pl.BlockSpec(shape, lambda b, t: zeros, pipeline_mode=pl.Buffered(1))


@jax.jit
def kernel(x, norm_in, w_in, w_pool, pool_scale, w_a, w_b, w_o, norm_out):
    batch, seq, d = x.shape
    assert d == D_MODEL and seq % TILE == 0 and TILE % CHUNK == 0
    nt = seq // TILE
    cosb, sinb, cosr, sinr = _rope_tables(nt)
    dmask, xi, zeta = _decay_tables()
    row = lambda v: v.reshape(1, -1).astype(F32)
    tok_spec = pl.BlockSpec((TILE, D_MODEL), lambda b, t: (b * nt + t, 0))
    base_spec = pl.BlockSpec((1, 1, HALF), lambda b, t: (t, 0, 0))
    out = pl.pallas_call(
        _block_kernel,
        grid=(batch, nt),
        in_specs=[
            tok_spec, base_spec, base_spec, _resident((TILE, HALF)), _resident((TILE, HALF)),
            _resident((1, D_MODEL)), _resident((1, D_MODEL)), _resident((1, D_MODEL)),
            _resident(dmask.shape), _resident(xi.shape), _resident(zeta.shape),
            _resident(_packed_shape(w_in)), _resident(_packed_shape(w_pool)),
            _resident(_packed_shape(w_a)), _resident(_packed_shape(w_b)), _resident(_packed_shape(w_o)),
        ],
        out_specs=tok_spec,
        out_shape=jax.ShapeDtypeStruct((batch * seq, D_MODEL), x.dtype),
        scratch_shapes=[
            pltpu.VMEM((POOL_HALO + TILE, D_MODEL), F32),
            pltpu.VMEM((HEADS, DK, DV), F32),
            pltpu.VMEM((TILE, HEADS * DV), BF16),
        ],
        compiler_params=pltpu.CompilerParams(
            dimension_semantics=("arbitrary", "arbitrary"),
            vmem_limit_bytes=VMEM_LIMIT_BYTES),
        name="hybrid_block",
    )(x.reshape(batch * seq, D_MODEL), cosb, sinb, cosr, sinr,
      row(norm_in), row(norm_out), row(pool_scale), dmask, xi, zeta,
      _pack_rows(w_in), _pack_rows(w_pool), _pack_rows(w_a), _pack_rows(w_b), _pack_rows(w_o))
    return out.reshape(batch, seq, D_MODEL)
```

```python
import jax
import jax.numpy as jnp
import numpy as np
from jax import lax
from jax.experimental import pallas as pl
from jax.experimental.pallas import tpu as pltpu

D_MODEL = 1024
CHUNK = 64
POOL_WINDOWS = (2, 4, 8, 16)
POOL_GROUP = D_MODEL // len(POOL_WINDOWS)
POOL_HALO = max(POOL_WINDOWS)
HEADS = 4
DK = D_MODEL // HEADS
DV = 2 * D_MODEL // HEADS
HALF = DK // 2
LANES = 128
ROPE_BASE = 10000.0
EPS = 1e-6

OFF_A_IN, OFF_A_GATE, OFF_Q, OFF_K, OFF_V, OFF_RGATE, OFF_GA, OFF_GB, D_IN = (
    0, 1024, 2048, 3072, 4096, 6144, 8192, 9216, 10240)

TILE = 256
STAGE_BYTES = 1 << 20
VMEM_LIMIT_BYTES = 58 * 1024 * 1024

F32 = jnp.float32
BF16 = jnp.bfloat16


def _gammas():
    return [1.0 - 2.0 ** (-5.0 - h) for h in range(HEADS)]


def _silu(v):
    return v * jax.nn.sigmoid(v)


def _stage_rows(k, n):
    rows = max(16, min(k, STAGE_BYTES // (4 * n)) // 16 * 16)
    assert k % rows == 0
    return rows


def _load_cast_weight(w_hbm, w_vmem, stage, sem):
    k, n = w_hbm.shape
    _, rows, _ = stage.shape
    nchunks = k // rows

    def chunk_copy(i, slot):
        return pltpu.make_async_copy(w_hbm.at[pl.ds(i * rows, rows), :], stage.at[slot], sem.at[slot])

    chunk_copy(0, 0).start()

    def body(i, carry):
        slot = lax.rem(i, 2)

        @pl.when(i + 1 < nchunks)
        def _prefetch():
            chunk_copy(i + 1, 1 - slot).start()

        chunk_copy(i, slot).wait()
        r0 = pl.multiple_of(i * rows, 16)
        w_vmem[pl.ds(r0, rows), :] = stage[slot].astype(BF16)
        return carry

    lax.fori_loop(0, nchunks, body, 0)


def _block_kernel(x_ref, cosb_ref, sinb_ref, cosr_ref, sinr_ref,
                  nin_ref, nout_ref, pscale_ref, dmask_ref, xi_ref, zeta_ref,
                  w_in_hbm, w_pool_hbm, w_a_hbm, w_b_hbm, w_o_hbm,
                  o_ref,
                  w_in_s, w_pool_s, w_a_s, w_b_s, w_o_s, stage_in, stage_pool, stage_sq, sem,
                  ubuf, state, ybin):
    t = pl.program_id(1)

    @pl.when((pl.program_id(0) == 0) & (t == 0))
    def _load_weights():
        _load_cast_weight(w_in_hbm, w_in_s, stage_in, sem)
        _load_cast_weight(w_pool_hbm, w_pool_s, stage_pool, sem)
        for w_hbm, w_vmem in ((w_a_hbm, w_a_s), (w_b_hbm, w_b_s), (w_o_hbm, w_o_s)):
            _load_cast_weight(w_hbm, w_vmem, stage_sq, sem)

    @pl.when(t == 0)
    def _reset_carries():
        ubuf[0:POOL_HALO, :] = jnp.zeros((POOL_HALO, D_MODEL), F32)
        state[...] = jnp.zeros(state.shape, F32)

    x = x_ref[...]
    h = x * lax.rsqrt(jnp.mean(x * x, axis=-1, keepdims=True) + EPS) * nin_ref[...]
    hb = h.astype(BF16)

    def proj(lo, hi):
        return jnp.dot(hb, w_in_s[:, lo:hi], preferred_element_type=F32)

    ubuf[POOL_HALO:POOL_HALO + TILE, :] = proj(OFF_A_IN, OFF_A_GATE)
    q_all = proj(OFF_Q, OFF_K)
    k_all = proj(OFF_K, OFF_V)
    v_all = proj(OFF_V, OFF_RGATE)
    a_gate = proj(OFF_A_GATE, OFF_Q)
    rg_all = proj(OFF_RGATE, OFF_GA)
    g_a = proj(OFF_GA, OFF_GB)

    pos1 = lax.broadcasted_iota(jnp.int32, (TILE, POOL_GROUP), 0) + (t * TILE + 1)
    mixed = []
    for g, w in enumerate(POOL_WINDOWS):
        cols = slice(g * POOL_GROUP, (g + 1) * POOL_GROUP)
        ext = ubuf[0:POOL_HALO + TILE, cols]
        win = ext
        span = 1
        while span < w:
            win = win + pltpu.roll(win, span, axis=0)
            span *= 2
        win = win[POOL_HALO:, :]
        u_g = ext[POOL_HALO:, :]
        cnt = jnp.minimum(pos1, w).astype(F32)
        pooled = win / cnt - u_g
        mixed.append(jnp.dot(pooled.astype(BF16), w_pool_s[g * POOL_GROUP:(g + 1) * POOL_GROUP, :],
                             preferred_element_type=F32))
    ubuf[0:POOL_HALO, :] = ubuf[TILE:TILE + POOL_HALO, :]
    ya_in = jnp.concatenate(mixed, axis=1) * pscale_ref[...] * _silu(a_gate)
    ya = jnp.dot(ya_in.astype(BF16), w_a_s[...], preferred_element_type=F32)
    merged = jax.nn.sigmoid(g_a) * ya

    cb, sb = cosb_ref[0], sinb_ref[0]
    cr, sr = cosr_ref[...], sinr_ref[...]
    cos = cb * cr - sb * sr
    sin = sb * cr + cb * sr

    def rotary(m, hd):
        m1 = m[:, hd * DK:hd * DK + HALF]
        m2 = m[:, hd * DK + HALF:(hd + 1) * DK]
        return jnp.concatenate([m1 * cos - m2 * sin, m1 * sin + m2 * cos], axis=1)

    gam = _gammas()
    vbs, ps, crosses = [], [], []
    for hd in range(HEADS):
        qb = (rotary(q_all, hd) * (DK ** -0.5)).astype(BF16)
        kr = rotary(k_all, hd)
        vb = v_all[:, hd * DV:(hd + 1) * DV].astype(BF16)
        scores = lax.dot_general(qb, kr.astype(BF16), (((1,), (1,)), ((), ())),
                                 preferred_element_type=F32)
        st = state[hd]
        xi = xi_ref[hd]
        crosses.append(jnp.dot(qb, st.astype(BF16), preferred_element_type=F32)
                       * jnp.concatenate([xi] * (DV // LANES), axis=1))
        zeta = zeta_ref[hd]
        kz = (kr * jnp.concatenate([zeta] * (DK // LANES), axis=1)).astype(BF16)
        upd = lax.dot_general(kz, vb, (((0,), (0,)), ((), ())), preferred_element_type=F32)
        state[hd] = st * (gam[hd] ** TILE) + upd
        ps.append((scores * dmask_ref[hd]).astype(BF16))
        vbs.append(vb)
    for hd in range(HEADS):
        o = jnp.dot(ps[hd], vbs[hd], preferred_element_type=F32) + crosses[hd]
        on = o * lax.rsqrt(jnp.mean(o * o, axis=-1, keepdims=True) + EPS)
        rg = rg_all[:, hd * DV:(hd + 1) * DV]
        ybin[:, hd * DV:(hd + 1) * DV] = (on * _silu(rg)).astype(BF16)
    g_b = proj(OFF_GB, D_IN)
    yb = jnp.dot(ybin[...], w_b_s[...], preferred_element_type=F32)

    merged = merged + jax.nn.sigmoid(g_b) * yb
    y = x + jnp.dot(merged.astype(BF16), w_o_s[...], preferred_element_type=F32)
    o_ref[...] = y * lax.rsqrt(jnp.mean(y * y, axis=-1, keepdims=True) + EPS) * nout_ref[...]


def _decay_tables():
    log_g = np.log(np.asarray(_gammas(), np.float64))
    idx = np.arange(TILE)
    diff = idx[:, None] - idx[None, :]
    same = (idx[:, None] // CHUNK) == (idx[None, :] // CHUNK)
    dist = np.where(same, np.abs(diff), diff).astype(np.float64)
    keep = same | (diff > 0)
    dmask = np.where(keep[None], np.exp(log_g[:, None, None] * dist[None]), 0.0)
    xi = np.exp(log_g[:, None] * (idx[None, :] + 1.0))
    zeta = np.exp(log_g[:, None] * (TILE - 1.0 - idx[None, :]))
    rep = lambda a: np.ascontiguousarray(
        np.broadcast_to(a[:, :, None], (HEADS, TILE, LANES)), dtype=np.float32)
    return dmask.astype(np.float32), rep(xi), rep(zeta)


def _rope_tables(nt):
    inv_freq = 1.0 / (ROPE_BASE ** (jnp.arange(HALF, dtype=F32) / HALF))
    base = (jnp.arange(nt, dtype=jnp.int32) * TILE).astype(F32)[:, None] * inv_freq[None, :]
    rel = jnp.arange(TILE, dtype=jnp.int32).astype(F32)[:, None] * inv_freq[None, :]
    base = base.reshape(nt, 1, HALF)
    return jnp.cos(base), jnp.sin(base), jnp.cos(rel), jnp.sin(rel)


def _resident(shape):
    zeros = (0,) * len(shape)
    return pl.BlockSpec(shape, lambda b, t: zeros, pipeline_mode=pl.Buffered(1))


@jax.jit
def kernel(x, norm_in, w_in, w_pool, pool_scale, w_a, w_b, w_o, norm_out):
    batch, seq, d = x.shape
    assert d == D_MODEL and seq % TILE == 0 and TILE % CHUNK == 0
    nt = seq // TILE
    cosb, sinb, cosr, sinr = _rope_tables(nt)
    dmask, xi, zeta = _decay_tables()
    row = lambda v: v.reshape(1, -1).astype(F32)
    w_pool2d = w_pool.reshape(-1, POOL_GROUP)
    weights = (w_in, w_pool2d, w_a, w_b, w_o)
    tok_spec = pl.BlockSpec((TILE, D_MODEL), lambda b, t: (b * nt + t, 0))
    base_spec = pl.BlockSpec((1, 1, HALF), lambda b, t: (t, 0, 0))
    hbm_spec = pl.BlockSpec(memory_space=pl.ANY)
    stage_buf = lambda w: pltpu.VMEM((2, _stage_rows(*w.shape), w.shape[1]), F32)
    assert w_b.shape[1] == w_a.shape[1] == w_o.shape[1]
    out = pl.pallas_call(
        _block_kernel,
        grid=(batch, nt),
        in_specs=[
            tok_spec, base_spec, base_spec, _resident((TILE, HALF)), _resident((TILE, HALF)),
            _resident((1, D_MODEL)), _resident((1, D_MODEL)), _resident((1, D_MODEL)),
            _resident(dmask.shape), _resident(xi.shape), _resident(zeta.shape),
        ] + [hbm_spec] * len(weights),
        out_specs=tok_spec,
        out_shape=jax.ShapeDtypeStruct((batch * seq, D_MODEL), x.dtype),
        scratch_shapes=[pltpu.VMEM(w.shape, BF16) for w in weights] + [
            stage_buf(w_in), stage_buf(w_pool2d), stage_buf(w_a),
            pltpu.SemaphoreType.DMA((2,)),
            pltpu.VMEM((POOL_HALO + TILE, D_MODEL), F32),
            pltpu.VMEM((HEADS, DK, DV), F32),
            pltpu.VMEM((TILE, HEADS * DV), BF16),
        ],
        compiler_params=pltpu.CompilerParams(
            dimension_semantics=("arbitrary", "arbitrary"),
            vmem_limit_bytes=VMEM_LIMIT_BYTES),
        name="hybrid_block",
    )(x.reshape(batch * seq, D_MODEL), cosb, sinb, cosr, sinr,
      row(norm_in), row(norm_out), row(pool_scale), dmask, xi, zeta, *weights)
    return out.reshape(batch, seq, D_MODEL)
```

```python
import jax
import jax.numpy as jnp
import numpy as np
from jax import lax
from jax.experimental import pallas as pl
from jax.experimental.pallas import tpu as pltpu

D_MODEL = 1024
CHUNK = 64
POOL_WINDOWS = (2, 4, 8, 16)
POOL_GROUP = D_MODEL // len(POOL_WINDOWS)
POOL_HALO = max(POOL_WINDOWS)
HEADS = 4
DK = D_MODEL // HEADS
DV = 2 * D_MODEL // HEADS
HALF = DK // 2
LANES = 128
ROPE_BASE = 10000.0
EPS = 1e-6

OFF_A_IN, OFF_A_GATE, OFF_Q, OFF_K, OFF_V, OFF_RGATE, OFF_GA, OFF_GB, D_IN = (
    0, 1024, 2048, 3072, 4096, 6144, 8192, 9216, 10240)

TILE = 256
STAGE_ROWS, STAGE_COLS = 256, 1024
STAGE_SLOTS = 6
VMEM_LIMIT_BYTES = 58 * 1024 * 1024

F32 = jnp.float32
BF16 = jnp.bfloat16


def _gammas():
    return [1.0 - 2.0 ** (-5.0 - h) for h in range(HEADS)]


def _silu(v):
    return v * jax.nn.sigmoid(v)


def _weight_blocks(pairs, rows, cols):
    blocks = []
    for w_hbm, w_vmem in pairs:
        k, n = w_hbm.shape
        assert k % rows == 0 and n % cols == 0
        blocks += [(w_hbm, w_vmem, r0, c0) for r0 in range(0, k, rows) for c0 in range(0, n, cols)]
    return blocks


def _load_cast_weights(rings):
    plans = []
    for pairs, stage, sem in rings:
        nslot, rows, cols = stage.shape
        blocks = _weight_blocks(pairs, rows, cols)

        def chunk_copy(i, blocks=blocks, stage=stage, sem=sem, nslot=nslot, rows=rows, cols=cols):
            w_hbm, _, r0, c0 = blocks[i]
            return pltpu.make_async_copy(w_hbm.at[r0:r0 + rows, c0:c0 + cols],
                                         stage.at[i % nslot], sem.at[i % nslot])

        for i in range(min(nslot, len(blocks))):
            chunk_copy(i).start()
        plans.append((blocks, stage, nslot, rows, cols, chunk_copy))
    for blocks, stage, nslot, rows, cols, chunk_copy in plans:
        for i, (_, w_vmem, r0, c0) in enumerate(blocks):
            chunk_copy(i).wait()
            w_vmem[r0:r0 + rows, c0:c0 + cols] = stage[i % nslot].astype(BF16)
            if i + nslot < len(blocks):
                chunk_copy(i + nslot).start()


def _block_kernel(x_ref, cosb_ref, sinb_ref, cosr_ref, sinr_ref,
                  nin_ref, nout_ref, pscale_ref, dmask_ref, xi_ref, zeta_ref,
                  w_in_hbm, w_pool_hbm, w_a_hbm, w_b_hbm, w_o_hbm,
                  o_ref,
                  w_in_s, w_pool_s, w_a_s, w_b_s, w_o_s, stage, stage_pool, sem, sem_pool,
                  ubuf, state, ybin):
    t = pl.program_id(1)

    @pl.when((pl.program_id(0) == 0) & (t == 0))
    def _load_weights():
        wide = ((w_in_hbm, w_in_s), (w_a_hbm, w_a_s), (w_b_hbm, w_b_s), (w_o_hbm, w_o_s))
        _load_cast_weights([(wide, stage, sem), (((w_pool_hbm, w_pool_s),), stage_pool, sem_pool)])

    @pl.when(t == 0)
    def _reset_carries():
        ubuf[0:POOL_HALO, :] = jnp.zeros((POOL_HALO, D_MODEL), F32)
        state[...] = jnp.zeros(state.shape, F32)

    x = x_ref[...]
    h = x * lax.rsqrt(jnp.mean(x * x, axis=-1, keepdims=True) + EPS) * nin_ref[...]
    hb = h.astype(BF16)

    def proj(lo, hi):
        return jnp.dot(hb, w_in_s[:, lo:hi], preferred_element_type=F32)

    ubuf[POOL_HALO:POOL_HALO + TILE, :] = proj(OFF_A_IN, OFF_A_GATE)
    q_all = proj(OFF_Q, OFF_K)
    k_all = proj(OFF_K, OFF_V)
    v_all = proj(OFF_V, OFF_RGATE)
    a_gate = proj(OFF_A_GATE, OFF_Q)
    rg_all = proj(OFF_RGATE, OFF_GA)
    g_a = proj(OFF_GA, OFF_GB)

    pos1 = lax.broadcasted_iota(jnp.int32, (TILE, POOL_GROUP), 0) + (t * TILE + 1)
    mixed = []
    for g, w in enumerate(POOL_WINDOWS):
        cols = slice(g * POOL_GROUP, (g + 1) * POOL_GROUP)
        ext = ubuf[0:POOL_HALO + TILE, cols]
        win = ext
        span = 1
        while span < w:
            win = win + pltpu.roll(win, span, axis=0)
            span *= 2
        win = win[POOL_HALO:, :]
        u_g = ext[POOL_HALO:, :]
        cnt = jnp.minimum(pos1, w).astype(F32)
        pooled = win / cnt - u_g
        mixed.append(jnp.dot(pooled.astype(BF16), w_pool_s[g * POOL_GROUP:(g + 1) * POOL_GROUP, :],
                             preferred_element_type=F32))
    ubuf[0:POOL_HALO, :] = ubuf[TILE:TILE + POOL_HALO, :]
    ya_in = jnp.concatenate(mixed, axis=1) * pscale_ref[...] * _silu(a_gate)
    ya = jnp.dot(ya_in.astype(BF16), w_a_s[...], preferred_element_type=F32)
    merged = jax.nn.sigmoid(g_a) * ya

    cb, sb = cosb_ref[0], sinb_ref[0]
    cr, sr = cosr_ref[...], sinr_ref[...]
    cos = cb * cr - sb * sr
    sin = sb * cr + cb * sr

    def rotary(m, hd):
        m1 = m[:, hd * DK:hd * DK + HALF]
        m2 = m[:, hd * DK + HALF:(hd + 1) * DK]
        return jnp.concatenate([m1 * cos - m2 * sin, m1 * sin + m2 * cos], axis=1)

    gam = _gammas()
    vbs, ps, crosses = [], [], []
    for hd in range(HEADS):
        qb = (rotary(q_all, hd) * (DK ** -0.5)).astype(BF16)
        kr = rotary(k_all, hd)
        vb = v_all[:, hd * DV:(hd + 1) * DV].astype(BF16)
        scores = lax.dot_general(qb, kr.astype(BF16), (((1,), (1,)), ((), ())),
                                 preferred_element_type=F32)
        st = state[hd]
        xi = xi_ref[hd]
        crosses.append(jnp.dot(qb, st.astype(BF16), preferred_element_type=F32)
                       * jnp.concatenate([xi] * (DV // LANES), axis=1))
        zeta = zeta_ref[hd]
        kz = (kr * jnp.concatenate([zeta] * (DK // LANES), axis=1)).astype(BF16)
        upd = lax.dot_general(kz, vb, (((0,), (0,)), ((), ())), preferred_element_type=F32)
        state[hd] = st * (gam[hd] ** TILE) + upd
        ps.append((scores * dmask_ref[hd]).astype(BF16))
        vbs.append(vb)
    for hd in range(HEADS):
        o = jnp.dot(ps[hd], vbs[hd], preferred_element_type=F32) + crosses[hd]
        on = o * lax.rsqrt(jnp.mean(o * o, axis=-1, keepdims=True) + EPS)
        rg = rg_all[:, hd * DV:(hd + 1) * DV]
        ybin[:, hd * DV:(hd + 1) * DV] = (on * _silu(rg)).astype(BF16)
    g_b = proj(OFF_GB, D_IN)
    yb = jnp.dot(ybin[...], w_b_s[...], preferred_element_type=F32)

    merged = merged + jax.nn.sigmoid(g_b) * yb
    y = x + jnp.dot(merged.astype(BF16), w_o_s[...], preferred_element_type=F32)
    o_ref[...] = y * lax.rsqrt(jnp.mean(y * y, axis=-1, keepdims=True) + EPS) * nout_ref[...]


def _decay_tables():
    log_g = np.log(np.asarray(_gammas(), np.float64))
    idx = np.arange(TILE)
    diff = idx[:, None] - idx[None, :]
    same = (idx[:, None] // CHUNK) == (idx[None, :] // CHUNK)
    dist = np.where(same, np.abs(diff), diff).astype(np.float64)
    keep = same | (diff > 0)
    dmask = np.where(keep[None], np.exp(log_g[:, None, None] * dist[None]), 0.0)
    xi = np.exp(log_g[:, None] * (idx[None, :] + 1.0))
    zeta = np.exp(log_g[:, None] * (TILE - 1.0 - idx[None, :]))
    rep = lambda a: np.ascontiguousarray(
        np.broadcast_to(a[:, :, None], (HEADS, TILE, LANES)), dtype=np.float32)
    return dmask.astype(np.float32), rep(xi), rep(zeta)


def _rope_tables(nt):
    inv_freq = 1.0 / (ROPE_BASE ** (jnp.arange(HALF, dtype=F32) / HALF))
    base = (jnp.arange(nt, dtype=jnp.int32) * TILE).astype(F32)[:, None] * inv_freq[None, :]
    rel = jnp.arange(TILE, dtype=jnp.int32).astype(F32)[:, None] * inv_freq[None, :]
    base = base.reshape(nt, 1, HALF)
    return jnp.cos(base), jnp.sin(base), jnp.cos(rel), jnp.sin(rel)


def _resident(shape):
    zeros = (0,) * len(shape)
    return pl.BlockSpec(shape, lambda b, t: zeros, pipeline_mode=pl.Buffered(1))


@jax.jit
def kernel(x, norm_in, w_in, w_pool, pool_scale, w_a, w_b, w_o, norm_out):
    batch, seq, d = x.shape
    assert d == D_MODEL and seq % TILE == 0 and TILE % CHUNK == 0
    nt = seq // TILE
    cosb, sinb, cosr, sinr = _rope_tables(nt)
    dmask, xi, zeta = _decay_tables()
    row = lambda v: v.reshape(1, -1).astype(F32)
    w_pool2d = w_pool.reshape(-1, POOL_GROUP)
    weights = (w_in, w_pool2d, w_a, w_b, w_o)
    tok_spec = pl.BlockSpec((TILE, D_MODEL), lambda b, t: (b * nt + t, 0))
    base_spec = pl.BlockSpec((1, 1, HALF), lambda b, t: (t, 0, 0))
    hbm_spec = pl.BlockSpec(memory_space=pl.ANY)
    out = pl.pallas_call(
        _block_kernel,
        grid=(batch, nt),
        in_specs=[
            tok_spec, base_spec, base_spec, _resident((TILE, HALF)), _resident((TILE, HALF)),
            _resident((1, D_MODEL)), _resident((1, D_MODEL)), _resident((1, D_MODEL)),
            _resident(dmask.shape), _resident(xi.shape), _resident(zeta.shape),
        ] + [hbm_spec] * len(weights),
        out_specs=tok_spec,
        out_shape=jax.ShapeDtypeStruct((batch * seq, D_MODEL), x.dtype),
        scratch_shapes=[pltpu.VMEM(w.shape, BF16) for w in weights] + [
            pltpu.VMEM((STAGE_SLOTS, STAGE_ROWS, STAGE_COLS), F32),
            pltpu.VMEM((w_pool2d.shape[0] // STAGE_ROWS, STAGE_ROWS, POOL_GROUP), F32),
            pltpu.SemaphoreType.DMA((STAGE_SLOTS,)),
            pltpu.SemaphoreType.DMA((w_pool2d.shape[0] // STAGE_ROWS,)),
            pltpu.VMEM((POOL_HALO + TILE, D_MODEL), F32),
            pltpu.VMEM((HEADS, DK, DV), F32),
            pltpu.VMEM((TILE, HEADS * DV), BF16),
        ],
        compiler_params=pltpu.CompilerParams(
            dimension_semantics=("arbitrary", "arbitrary"),
            vmem_limit_bytes=VMEM_LIMIT_BYTES),
        name="hybrid_block",
    )(x.reshape(batch * seq, D_MODEL), cosb, sinb, cosr, sinr,
      row(norm_in), row(norm_out), row(pool_scale), dmask, xi, zeta, *weights)
    return out.reshape(batch, seq, D_MODEL)
```

```python
import jax
import jax.numpy as jnp
import numpy as np
from jax import lax
from jax.experimental import pallas as pl
from jax.experimental.pallas import tpu as pltpu

D_MODEL = 1024
CHUNK = 64
POOL_WINDOWS = (2, 4, 8, 16)
POOL_GROUP = D_MODEL // len(POOL_WINDOWS)
POOL_HALO = max(POOL_WINDOWS)
HEADS = 4
DK = D_MODEL // HEADS
DV = 2 * D_MODEL // HEADS
HALF = DK // 2
LANES = 128
ROPE_BASE = 10000.0
EPS = 1e-6

OFF_A_IN, OFF_A_GATE, OFF_Q, OFF_K, OFF_V, OFF_RGATE, OFF_GA, OFF_GB, D_IN = (
    0, 1024, 2048, 3072, 4096, 6144, 8192, 9216, 10240)

TILE = 256
STAGE_ROWS, STAGE_COLS = 256, 1024
STAGE_SLOTS = 8
VMEM_LIMIT_BYTES = 58 * 1024 * 1024

F32 = jnp.float32
BF16 = jnp.bfloat16


def _gammas():
    return [1.0 - 2.0 ** (-5.0 - h) for h in range(HEADS)]


def _silu(v):
    return v * jax.nn.sigmoid(v)


def _weight_blocks(pairs, rows, cols):
    blocks = []
    for w_hbm, w_vmem in pairs:
        k, n = w_hbm.shape
        assert k % rows == 0 and n % cols == 0
        blocks += [(w_hbm, w_vmem, r0, c0) for r0 in range(0, k, rows) for c0 in range(0, n, cols)]
    return blocks


def _load_cast_weights(rings):
    plans = []
    for pairs, stage, sem in rings:
        nslot, rows, cols = stage.shape
        blocks = _weight_blocks(pairs, rows, cols)

        def chunk_copy(i, blocks=blocks, stage=stage, sem=sem, nslot=nslot, rows=rows, cols=cols):
            w_hbm, _, r0, c0 = blocks[i]
            return pltpu.make_async_copy(w_hbm.at[r0:r0 + rows, c0:c0 + cols],
                                         stage.at[i % nslot], sem.at[i % nslot])

        for i in range(min(nslot, len(blocks))):
            chunk_copy(i).start(priority=i % 2)
        plans.append((blocks, stage, nslot, rows, cols, chunk_copy))
    for blocks, stage, nslot, rows, cols, chunk_copy in plans:
        for i, (_, w_vmem, r0, c0) in enumerate(blocks):
            chunk_copy(i).wait()
            w_vmem[r0:r0 + rows, c0:c0 + cols] = stage[i % nslot].astype(BF16)
            if i + nslot < len(blocks):
                chunk_copy(i + nslot).start(priority=i % 2)


def _block_kernel(x_ref, cosb_ref, sinb_ref, cosr_ref, sinr_ref,
                  nin_ref, nout_ref, pscale_ref, dmask_ref, xi_ref, zeta_ref,
                  w_in_hbm, w_pool_hbm, w_a_hbm, w_b_hbm, w_o_hbm,
                  o_ref,
                  w_in_s, w_pool_s, w_a_s, w_b_s, w_o_s, stage, stage_pool, sem, sem_pool,
                  ubuf, state, ybin):
    t = pl.program_id(1)

    @pl.when((pl.program_id(0) == 0) & (t == 0))
    def _load_weights():
        wide = ((w_in_hbm, w_in_s), (w_a_hbm, w_a_s), (w_b_hbm, w_b_s), (w_o_hbm, w_o_s))
        _load_cast_weights([(wide, stage, sem), (((w_pool_hbm, w_pool_s),), stage_pool, sem_pool)])

    @pl.when(t == 0)
    def _reset_carries():
        ubuf[0:POOL_HALO, :] = jnp.zeros((POOL_HALO, D_MODEL), F32)
        state[...] = jnp.zeros(state.shape, F32)

    x = x_ref[...]
    h = x * lax.rsqrt(jnp.mean(x * x, axis=-1, keepdims=True) + EPS) * nin_ref[...]
    hb = h.astype(BF16)

    def proj(lo, hi):
        return jnp.dot(hb, w_in_s[:, lo:hi], preferred_element_type=F32)

    cb, sb = cosb_ref[0], sinb_ref[0]
    cr, sr = cosr_ref[...], sinr_ref[...]
    cos = cb * cr - sb * sr
    sin = sb * cr + cb * sr

    def rotary(m, hd):
        m1 = m[:, hd * DK:hd * DK + HALF]
        m2 = m[:, hd * DK + HALF:(hd + 1) * DK]
        return jnp.concatenate([m1 * cos - m2 * sin, m1 * sin + m2 * cos], axis=1)

    ubuf[POOL_HALO:POOL_HALO + TILE, :] = proj(OFF_A_IN, OFF_A_GATE)
    q_all = proj(OFF_Q, OFF_K)
    qbs, qxs = [], []
    for hd in range(HEADS):
        qr = rotary(q_all, hd) * (DK ** -0.5)
        xi = xi_ref[hd]
        qbs.append(qr.astype(BF16))
        qxs.append((qr * jnp.concatenate([xi] * (DK // LANES), axis=1)).astype(BF16))
    k_all = proj(OFF_K, OFF_V)
    kbs, kzs = [], []
    for hd in range(HEADS):
        kr = rotary(k_all, hd)
        zeta = zeta_ref[hd]
        kbs.append(kr.astype(BF16))
        kzs.append((kr * jnp.concatenate([zeta] * (DK // LANES), axis=1)).astype(BF16))
    v_all = proj(OFF_V, OFF_RGATE)
    vbs = [v_all[:, hd * DV:(hd + 1) * DV].astype(BF16) for hd in range(HEADS)]
    a_gate = proj(OFF_A_GATE, OFF_Q)
    g_a = proj(OFF_GA, OFF_GB)

    pos1 = lax.broadcasted_iota(jnp.int32, (TILE, POOL_GROUP), 0) + (t * TILE + 1)
    mixed = []
    for g, w in enumerate(POOL_WINDOWS):
        cols = slice(g * POOL_GROUP, (g + 1) * POOL_GROUP)
        ext = ubuf[0:POOL_HALO + TILE, cols]
        win = ext
        span = 1
        while span < w:
            win = win + pltpu.roll(win, span, axis=0)
            span *= 2
        win = win[POOL_HALO:, :]
        u_g = ext[POOL_HALO:, :]
        cnt = jnp.minimum(pos1, w).astype(F32)
        pooled = win / cnt - u_g
        mixed.append(jnp.dot(pooled.astype(BF16), w_pool_s[g * POOL_GROUP:(g + 1) * POOL_GROUP, :],
                             preferred_element_type=F32))
    ubuf[0:POOL_HALO, :] = ubuf[TILE:TILE + POOL_HALO, :]
    ya_in = jnp.concatenate(mixed, axis=1) * pscale_ref[...] * _silu(a_gate)
    ya = jnp.dot(ya_in.astype(BF16), w_a_s[...], preferred_element_type=F32)
    merged = jax.nn.sigmoid(g_a) * ya

    gam = _gammas()
    ps, sts = [], []
    for hd in range(HEADS):
        scores = lax.dot_general(qbs[hd], kbs[hd], (((1,), (1,)), ((), ())),
                                 preferred_element_type=F32)
        st = state[hd]
        sts.append(st.astype(BF16))
        upd = lax.dot_general(kzs[hd], vbs[hd], (((0,), (0,)), ((), ())), preferred_element_type=F32)
        state[hd] = st * (gam[hd] ** TILE) + upd
        ps.append((scores * dmask_ref[hd]).astype(BF16))
    rg_all = proj(OFF_RGATE, OFF_GA)
    for hd in range(HEADS):
        lhs = jnp.concatenate([ps[hd], qxs[hd]], axis=1)
        rhs = jnp.concatenate([vbs[hd], sts[hd]], axis=0)
        o = jnp.dot(lhs, rhs, preferred_element_type=F32)
        on = o * lax.rsqrt(jnp.mean(o * o, axis=-1, keepdims=True) + EPS)
        rg = rg_all[:, hd * DV:(hd + 1) * DV]
        ybin[:, hd * DV:(hd + 1) * DV] = (on * _silu(rg)).astype(BF16)
    g_b = proj(OFF_GB, D_IN)
    yb = jnp.dot(ybin[...], w_b_s[...], preferred_element_type=F32)

    merged = merged + jax.nn.sigmoid(g_b) * yb
    y = x + jnp.dot(merged.astype(BF16), w_o_s[...], preferred_element_type=F32)
    o_ref[...] = y * lax.rsqrt(jnp.mean(y * y, axis=-1, keepdims=True) + EPS) * nout_ref[...]


def _decay_tables():
    log_g = np.log(np.asarray(_gammas(), np.float64))
    idx = np.arange(TILE)
    diff = idx[:, None] - idx[None, :]
    same = (idx[:, None] // CHUNK) == (idx[None, :] // CHUNK)
    dist = np.where(same, np.abs(diff), diff).astype(np.float64)
    keep = same | (diff > 0)
    dmask = np.where(keep[None], np.exp(log_g[:, None, None] * dist[None]), 0.0)
    xi = np.exp(log_g[:, None] * (idx[None, :] + 1.0))
    zeta = np.exp(log_g[:, None] * (TILE - 1.0 - idx[None, :]))
    rep = lambda a: np.ascontiguousarray(
        np.broadcast_to(a[:, :, None], (HEADS, TILE, LANES)), dtype=np.float32)
    return dmask.astype(np.float32), rep(xi), rep(zeta)


def _rope_tables(nt):
    inv_freq = 1.0 / (ROPE_BASE ** (jnp.arange(HALF, dtype=F32) / HALF))
    base = (jnp.arange(nt, dtype=jnp.int32) * TILE).astype(F32)[:, None] * inv_freq[None, :]
    rel = jnp.arange(TILE, dtype=jnp.int32).astype(F32)[:, None] * inv_freq[None, :]
    base = base.reshape(nt, 1, HALF)
    return jnp.cos(base), jnp.sin(base), jnp.cos(rel), jnp.sin(rel)


def _resident(shape):
    zeros = (0,) * len(shape)
    return pl.BlockSpec(shape, lambda b, t: zeros, pipeline_mode=pl.Buffered(1))


@jax.jit
def kernel(x, norm_in, w_in, w_pool, pool_scale, w_a, w_b, w_o, norm_out):
    batch, seq, d = x.shape
    assert d == D_MODEL and seq % TILE == 0 and TILE % CHUNK == 0
    nt = seq // TILE
    cosb, sinb, cosr, sinr = _rope_tables(nt)
    dmask, xi, zeta = _decay_tables()
    row = lambda v: v.reshape(1, -1).astype(F32)
    w_pool2d = w_pool.reshape(-1, POOL_GROUP)
    weights = (w_in, w_pool2d, w_a, w_b, w_o)
    tok_spec = pl.BlockSpec((TILE, D_MODEL), lambda b, t: (b * nt + t, 0))
    base_spec = pl.BlockSpec((1, 1, HALF), lambda b, t: (t, 0, 0))
    hbm_spec = pl.BlockSpec(memory_space=pl.ANY)
    out = pl.pallas_call(
        _block_kernel,
        grid=(batch, nt),
        in_specs=[
            tok_spec, base_spec, base_spec, _resident((TILE, HALF)), _resident((TILE, HALF)),
            _resident((1, D_MODEL)), _resident((1, D_MODEL)), _resident((1, D_MODEL)),
            _resident(dmask.shape), _resident(xi.shape), _resident(zeta.shape),
        ] + [hbm_spec] * len(weights),
        out_specs=tok_spec,
        out_shape=jax.ShapeDtypeStruct((batch * seq, D_MODEL), x.dtype),
        scratch_shapes=[pltpu.VMEM(w.shape, BF16) for w in weights] + [
            pltpu.VMEM((STAGE_SLOTS, STAGE_ROWS, STAGE_COLS), F32),
            pltpu.VMEM((w_pool2d.shape[0] // STAGE_ROWS, STAGE_ROWS, POOL_GROUP), F32),
            pltpu.SemaphoreType.DMA((STAGE_SLOTS,)),
            pltpu.SemaphoreType.DMA((w_pool2d.shape[0] // STAGE_ROWS,)),
            pltpu.VMEM((POOL_HALO + TILE, D_MODEL), F32),
            pltpu.VMEM((HEADS, DK, DV), F32),
            pltpu.VMEM((TILE, HEADS * DV), BF16),
        ],
        compiler_params=pltpu.CompilerParams(
            dimension_semantics=("arbitrary", "arbitrary"),
            vmem_limit_bytes=VMEM_LIMIT_BYTES),
        name="hybrid_block",
    )(x.reshape(batch * seq, D_MODEL), cosb, sinb, cosr, sinr,
      row(norm_in), row(norm_out), row(pool_scale), dmask, xi, zeta, *weights)
    return out.reshape(batch, seq, D_MODEL)
```

```python
import functools

import jax
import jax.numpy as jnp
import numpy as np
from jax import lax
from jax.experimental import pallas as pl
from jax.experimental.pallas import tpu as pltpu

D_MODEL = 1024
CHUNK = 64
POOL_WINDOWS = (2, 4, 8, 16)
POOL_GROUP = D_MODEL // len(POOL_WINDOWS)
POOL_HALO = max(POOL_WINDOWS)
HEADS = 4
DK = D_MODEL // HEADS
DV = 2 * D_MODEL // HEADS
HALF = DK // 2
LANES = 128
ROPE_BASE = 10000.0
EPS = 1e-6

OFF_A_IN, OFF_A_GATE, OFF_Q, OFF_K, OFF_V, OFF_RGATE, OFF_GA, OFF_GB, D_IN = (
    0, 1024, 2048, 3072, 4096, 6144, 8192, 9216, 10240)

TILE = 256
STAGE_ROWS, STAGE_COLS = 256, 1024
STAGE_SLOTS = 8
VMEM_LIMIT_BYTES = 58 * 1024 * 1024

F32 = jnp.float32
BF16 = jnp.bfloat16


def _gammas():
    return [1.0 - 2.0 ** (-5.0 - h) for h in range(HEADS)]


def _silu(v):
    return v * jax.nn.sigmoid(v)


def _weight_blocks(pairs, rows, cols):
    blocks = []
    for w_hbm, w_vmem in pairs:
        k, n = w_hbm.shape
        assert k % rows == 0 and n % cols == 0
        blocks += [(w_hbm, w_vmem, r0, c0) for r0 in range(0, k, rows) for c0 in range(0, n, cols)]
    return blocks


def _load_cast_weights(rings):
    plans = []
    for pairs, stage, sem in rings:
        nslot, rows, cols = stage.shape
        blocks = _weight_blocks(pairs, rows, cols)

        def chunk_copy(i, blocks=blocks, stage=stage, sem=sem, nslot=nslot, rows=rows, cols=cols):
            w_hbm, _, r0, c0 = blocks[i]
            return pltpu.make_async_copy(w_hbm.at[r0:r0 + rows, c0:c0 + cols],
                                         stage.at[i % nslot], sem.at[i % nslot])

        for i in range(min(nslot, len(blocks))):
            chunk_copy(i).start(priority=i % 2)
        plans.append((blocks, stage, nslot, rows, cols, chunk_copy))
    for blocks, stage, nslot, rows, cols, chunk_copy in plans:
        for i, (_, w_vmem, r0, c0) in enumerate(blocks):
            chunk_copy(i).wait()
            w_vmem[r0:r0 + rows, c0:c0 + cols] = stage[i % nslot].astype(BF16)
            if i + nslot < len(blocks):
                chunk_copy(i + nslot).start(priority=i % 2)


def _block_kernel(x_ref, cosb_ref, sinb_ref, cosr_ref, sinr_ref,
                  nin_ref, nout_ref, pscale_ref, dmask_ref, xi_ref, zeta_ref,
                  w_in_hbm, w_pool_hbm, w_a_hbm, w_b_hbm, w_o_hbm,
                  o_ref,
                  w_in_s, w_pool_s, w_a_s, w_b_s, w_o_s, stage, stage_pool, sem, sem_pool,
                  ubuf, state, ybin, pend_m, pend_x, *, tiles_per_seq, num_tiles):
    s = pl.program_id(0)
    t = lax.rem(s, tiles_per_seq)

    @pl.when(s == 0)
    def _first_step():
        wide = ((w_in_hbm, w_in_s), (w_a_hbm, w_a_s), (w_b_hbm, w_b_s), (w_o_hbm, w_o_s))
        _load_cast_weights([(wide, stage, sem), (((w_pool_hbm, w_pool_s),), stage_pool, sem_pool)])
        pend_m[...] = jnp.zeros(pend_m.shape, BF16)
        pend_x[...] = jnp.zeros(pend_x.shape, F32)

    @pl.when(t == 0)
    def _reset_carries():
        ubuf[0:POOL_HALO, :] = jnp.zeros((POOL_HALO, D_MODEL), F32)
        state[...] = jnp.zeros(state.shape, F32)

    def finish_pending():
        y = pend_x[...] + jnp.dot(pend_m[...], w_o_s[...], preferred_element_type=F32)
        o_ref[...] = y * lax.rsqrt(jnp.mean(y * y, axis=-1, keepdims=True) + EPS) * nout_ref[...]

    @pl.when(s < num_tiles)
    def _tile():
        finish_pending()
        _compute_tile(t, x_ref, cosb_ref, sinb_ref, cosr_ref, sinr_ref, nin_ref, pscale_ref,
                      dmask_ref, xi_ref, zeta_ref, w_in_s, w_pool_s, w_a_s, w_b_s,
                      ubuf, state, ybin, pend_m, pend_x)

    @pl.when(s == num_tiles)
    def _drain():
        finish_pending()


def _compute_tile(t, x_ref, cosb_ref, sinb_ref, cosr_ref, sinr_ref, nin_ref, pscale_ref,
                  dmask_ref, xi_ref, zeta_ref, w_in_s, w_pool_s, w_a_s, w_b_s,
                  ubuf, state, ybin, pend_m, pend_x):
    x = x_ref[...]
    h = x * lax.rsqrt(jnp.mean(x * x, axis=-1, keepdims=True) + EPS) * nin_ref[...]
    hb = h.astype(BF16)

    def proj(lo, hi):
        return jnp.dot(hb, w_in_s[:, lo:hi], preferred_element_type=F32)

    cb, sb = cosb_ref[0], sinb_ref[0]
    cr, sr = cosr_ref[...], sinr_ref[...]
    cos = cb * cr - sb * sr
    sin = sb * cr + cb * sr

    def rotary(m, hd):
        m1 = m[:, hd * DK:hd * DK + HALF]
        m2 = m[:, hd * DK + HALF:(hd + 1) * DK]
        return jnp.concatenate([m1 * cos - m2 * sin, m1 * sin + m2 * cos], axis=1)

    ubuf[POOL_HALO:POOL_HALO + TILE, :] = proj(OFF_A_IN, OFF_A_GATE)
    q_all = proj(OFF_Q, OFF_K)
    qbs, qxs = [], []
    for hd in range(HEADS):
        qr = rotary(q_all, hd) * (DK ** -0.5)
        xi = xi_ref[hd]
        qbs.append(qr.astype(BF16))
        qxs.append((qr * jnp.concatenate([xi] * (DK // LANES), axis=1)).astype(BF16))
    k_all = proj(OFF_K, OFF_V)
    kbs, kzs = [], []
    for hd in range(HEADS):
        kr = rotary(k_all, hd)
        zeta = zeta_ref[hd]
        kbs.append(kr.astype(BF16))
        kzs.append((kr * jnp.concatenate([zeta] * (DK // LANES), axis=1)).astype(BF16))
    v_all = proj(OFF_V, OFF_RGATE)
    vbs = [v_all[:, hd * DV:(hd + 1) * DV].astype(BF16) for hd in range(HEADS)]
    a_gate = proj(OFF_A_GATE, OFF_Q)
    g_a = proj(OFF_GA, OFF_GB)

    pos1 = lax.broadcasted_iota(jnp.int32, (TILE, POOL_GROUP), 0) + (t * TILE + 1)
    mixed = []
    for g, w in enumerate(POOL_WINDOWS):
        cols = slice(g * POOL_GROUP, (g + 1) * POOL_GROUP)
        ext = ubuf[0:POOL_HALO + TILE, cols]
        win = ext
        span = 1
        while span < w:
            win = win + pltpu.roll(win, span, axis=0)
            span *= 2
        win = win[POOL_HALO:, :]
        u_g = ext[POOL_HALO:, :]
        cnt = jnp.minimum(pos1, w).astype(F32)
        pooled = win / cnt - u_g
        mixed.append(jnp.dot(pooled.astype(BF16), w_pool_s[g * POOL_GROUP:(g + 1) * POOL_GROUP, :],
                             preferred_element_type=F32))
    ubuf[0:POOL_HALO, :] = ubuf[TILE:TILE + POOL_HALO, :]
    ya_in = jnp.concatenate(mixed, axis=1) * pscale_ref[...] * _silu(a_gate)
    ya = jnp.dot(ya_in.astype(BF16), w_a_s[...], preferred_element_type=F32)
    merged = jax.nn.sigmoid(g_a) * ya

    gam = _gammas()
    ps, sts = [], []
    for hd in range(HEADS):
        scores = lax.dot_general(qbs[hd], kbs[hd], (((1,), (1,)), ((), ())),
                                 preferred_element_type=F32)
        st = state[hd]
        sts.append(st.astype(BF16))
        upd = lax.dot_general(kzs[hd], vbs[hd], (((0,), (0,)), ((), ())), preferred_element_type=F32)
        state[hd] = st * (gam[hd] ** TILE) + upd
        ps.append((scores * dmask_ref[hd]).astype(BF16))
    rg_all = proj(OFF_RGATE, OFF_GA)
    for hd in range(HEADS):
        lhs = jnp.concatenate([ps[hd], qxs[hd]], axis=1)
        rhs = jnp.concatenate([vbs[hd], sts[hd]], axis=0)
        o = jnp.dot(lhs, rhs, preferred_element_type=F32)
        on = o * lax.rsqrt(jnp.mean(o * o, axis=-1, keepdims=True) + EPS)
        rg = rg_all[:, hd * DV:(hd + 1) * DV]
        ybin[:, hd * DV:(hd + 1) * DV] = (on * _silu(rg)).astype(BF16)
    g_b = proj(OFF_GB, D_IN)
    yb = jnp.dot(ybin[...], w_b_s[...], preferred_element_type=F32)

    pend_m[...] = (merged + jax.nn.sigmoid(g_b) * yb).astype(BF16)
    pend_x[...] = x


def _decay_tables():
    log_g = np.log(np.asarray(_gammas(), np.float64))
    idx = np.arange(TILE)
    diff = idx[:, None] - idx[None, :]
    same = (idx[:, None] // CHUNK) == (idx[None, :] // CHUNK)
    dist = np.where(same, np.abs(diff), diff).astype(np.float64)
    keep = same | (diff > 0)
    dmask = np.where(keep[None], np.exp(log_g[:, None, None] * dist[None]), 0.0)
    xi = np.exp(log_g[:, None] * (idx[None, :] + 1.0))
    zeta = np.exp(log_g[:, None] * (TILE - 1.0 - idx[None, :]))
    rep = lambda a: np.ascontiguousarray(
        np.broadcast_to(a[:, :, None], (HEADS, TILE, LANES)), dtype=np.float32)
    return dmask.astype(np.float32), rep(xi), rep(zeta)


def _rope_tables(nt):
    inv_freq = 1.0 / (ROPE_BASE ** (jnp.arange(HALF, dtype=F32) / HALF))
    base = (jnp.arange(nt, dtype=jnp.int32) * TILE).astype(F32)[:, None] * inv_freq[None, :]
    rel = jnp.arange(TILE, dtype=jnp.int32).astype(F32)[:, None] * inv_freq[None, :]
    base = base.reshape(nt, 1, HALF)
    return jnp.cos(base), jnp.sin(base), jnp.cos(rel), jnp.sin(rel)


def _resident(shape):
    zeros = (0,) * len(shape)
    return pl.BlockSpec(shape, lambda s: zeros, pipeline_mode=pl.Buffered(1))


@jax.jit
def kernel(x, norm_in, w_in, w_pool, pool_scale, w_a, w_b, w_o, norm_out):
    batch, seq, d = x.shape
    assert d == D_MODEL and seq % TILE == 0 and TILE % CHUNK == 0
    nt = seq // TILE
    cosb, sinb, cosr, sinr = _rope_tables(nt)
    dmask, xi, zeta = _decay_tables()
    row = lambda v: v.reshape(1, -1).astype(F32)
    w_pool2d = w_pool.reshape(-1, POOL_GROUP)
    weights = (w_in, w_pool2d, w_a, w_b, w_o)
    num_tiles = batch * nt
    cur = lambda s: jnp.minimum(s, num_tiles - 1)
    in_spec = pl.BlockSpec((TILE, D_MODEL), lambda s: (cur(s), 0))
    out_spec = pl.BlockSpec((TILE, D_MODEL), lambda s: (jnp.maximum(s - 1, 0), 0))
    base_spec = pl.BlockSpec((1, 1, HALF), lambda s: (lax.rem(cur(s), nt), 0, 0))
    hbm_spec = pl.BlockSpec(memory_space=pl.ANY)
    out = pl.pallas_call(
        functools.partial(_block_kernel, tiles_per_seq=nt, num_tiles=num_tiles),
        grid=(num_tiles + 1,),
        in_specs=[
            in_spec, base_spec, base_spec, _resident((TILE, HALF)), _resident((TILE, HALF)),
            _resident((1, D_MODEL)), _resident((1, D_MODEL)), _resident((1, D_MODEL)),
            _resident(dmask.shape), _resident(xi.shape), _resident(zeta.shape),
        ] + [hbm_spec] * len(weights),
        out_specs=out_spec,
        out_shape=jax.ShapeDtypeStruct((batch * seq, D_MODEL), x.dtype),
        scratch_shapes=[pltpu.VMEM(w.shape, BF16) for w in weights] + [
            pltpu.VMEM((STAGE_SLOTS, STAGE_ROWS, STAGE_COLS), F32),
            pltpu.VMEM((w_pool2d.shape[0] // STAGE_ROWS, STAGE_ROWS, POOL_GROUP), F32),
            pltpu.SemaphoreType.DMA((STAGE_SLOTS,)),
            pltpu.SemaphoreType.DMA((w_pool2d.shape[0] // STAGE_ROWS,)),
            pltpu.VMEM((POOL_HALO + TILE, D_MODEL), F32),
            pltpu.VMEM((HEADS, DK, DV), F32),
            pltpu.VMEM((TILE, HEADS * DV), BF16),
            pltpu.VMEM((TILE, D_MODEL), BF16),
            pltpu.VMEM((TILE, D_MODEL), F32),
        ],
        compiler_params=pltpu.CompilerParams(
            dimension_semantics=("arbitrary",),
            vmem_limit_bytes=VMEM_LIMIT_BYTES),
        name="hybrid_block",
    )(x.reshape(batch * seq, D_MODEL), cosb, sinb, cosr, sinr,
      row(norm_in), row(norm_out), row(pool_scale), dmask, xi, zeta, *weights)
    return out.reshape(batch, seq, D_MODEL)
```

```python
import functools

import jax
import jax.numpy as jnp
import numpy as np
from jax import lax
from jax.experimental import pallas as pl
from jax.experimental.pallas import tpu as pltpu

D_MODEL = 1024
CHUNK = 64
POOL_WINDOWS = (2, 4, 8, 16)
POOL_GROUP = D_MODEL // len(POOL_WINDOWS)
POOL_HALO = max(POOL_WINDOWS)
HEADS = 4
DK = D_MODEL // HEADS
DV = 2 * D_MODEL // HEADS
HALF = DK // 2
LANES = 128
ROPE_BASE = 10000.0
EPS = 1e-6

OFF_A_IN, OFF_A_GATE, OFF_Q, OFF_K, OFF_V, OFF_RGATE, OFF_GA, OFF_GB, D_IN = (
    0, 1024, 2048, 3072, 4096, 6144, 8192, 9216, 10240)

TILE = 256
STAGE_ROWS, STAGE_COLS = 256, 1024
STAGE_SLOTS = 5
VMEM_LIMIT_BYTES = 58 * 1024 * 1024

F32 = jnp.float32
BF16 = jnp.bfloat16


def _gammas():
    return [1.0 - 2.0 ** (-5.0 - h) for h in range(HEADS)]


def _silu(v):
    return v * jax.nn.sigmoid(v)


def _weight_blocks(pairs, rows, cols):
    blocks = []
    for w_hbm, w_vmem in pairs:
        k, n = w_hbm.shape
        assert k % rows == 0 and n % cols == 0
        blocks += [(w_hbm, w_vmem, r0, c0) for r0 in range(0, k, rows) for c0 in range(0, n, cols)]
    return blocks


def _load_cast_weights(rings):
    plans = []
    for pairs, stage, sem in rings:
        nslot, rows, cols = stage.shape
        blocks = _weight_blocks(pairs, rows, cols)

        def chunk_copy(i, blocks=blocks, stage=stage, sem=sem, nslot=nslot, rows=rows, cols=cols):
            w_hbm, _, r0, c0 = blocks[i]
            return pltpu.make_async_copy(w_hbm.at[r0:r0 + rows, c0:c0 + cols],
                                         stage.at[i % nslot], sem.at[i % nslot])

        for i in range(min(nslot, len(blocks))):
            chunk_copy(i).start(priority=i % 2)
        plans.append((blocks, stage, nslot, rows, cols, chunk_copy))
    for blocks, stage, nslot, rows, cols, chunk_copy in plans:
        for i, (_, w_vmem, r0, c0) in enumerate(blocks):
            chunk_copy(i).wait()
            w_vmem[r0:r0 + rows, c0:c0 + cols] = stage[i % nslot].astype(BF16)
            if i + nslot < len(blocks):
                chunk_copy(i + nslot).start(priority=i % 2)


def _block_kernel(x_ref, cosb_ref, sinb_ref, cosr_ref, sinr_ref,
                  nin_ref, nout_ref, pscale_ref, dmask_ref, xi_ref, zeta_ref,
                  w_in_hbm, w_pool_hbm, w_a_hbm, w_b_hbm, w_o_hbm,
                  o_ref,
                  w_in_s, w_pool_s, w_a_s, w_b_s, w_o_s, stage, stage_pool, sem, sem_pool,
                  ubuf, state, ybin, pend_m, pend_x, *, tiles_per_seq, num_tiles):
    s = pl.program_id(0)
    t = lax.rem(s, tiles_per_seq)

    @pl.when(s == 0)
    def _first_step():
        wide = ((w_in_hbm, w_in_s), (w_a_hbm, w_a_s), (w_b_hbm, w_b_s), (w_o_hbm, w_o_s))
        _load_cast_weights([(wide, stage, sem), (((w_pool_hbm, w_pool_s),), stage_pool, sem_pool)])
        pend_m[...] = jnp.zeros(pend_m.shape, BF16)
        pend_x[...] = jnp.zeros(pend_x.shape, F32)

    @pl.when(t == 0)
    def _reset_carries():
        ubuf[0:POOL_HALO, :] = jnp.zeros((POOL_HALO, D_MODEL), F32)
        state[...] = jnp.zeros(state.shape, F32)

    def finish_pending():
        y = pend_x[...] + jnp.dot(pend_m[...], w_o_s[...], preferred_element_type=F32)
        o_ref[...] = y * lax.rsqrt(jnp.mean(y * y, axis=-1, keepdims=True) + EPS) * nout_ref[...]

    @pl.when(s < num_tiles)
    def _tile():
        finish_pending()
        _compute_tile(t, x_ref, cosb_ref, sinb_ref, cosr_ref, sinr_ref, nin_ref, pscale_ref,
                      dmask_ref, xi_ref, zeta_ref, w_in_s, w_pool_s, w_a_s, w_b_s,
                      ubuf, state, ybin, pend_m, pend_x)

    @pl.when(s == num_tiles)
    def _drain():
        finish_pending()


def _compute_tile(t, x_ref, cosb_ref, sinb_ref, cosr_ref, sinr_ref, nin_ref, pscale_ref,
                  dmask_ref, xi_ref, zeta_ref, w_in_s, w_pool_s, w_a_s, w_b_s,
                  ubuf, state, ybin, pend_m, pend_x):
    x = x_ref[...]
    h = x * lax.rsqrt(jnp.mean(x * x, axis=-1, keepdims=True) + EPS) * nin_ref[...]
    hb = h.astype(BF16)

    def proj(lo, hi):
        return jnp.dot(hb, w_in_s[:, lo:hi], preferred_element_type=F32)

    cb, sb = cosb_ref[0], sinb_ref[0]
    cr, sr = cosr_ref[...], sinr_ref[...]
    cos = cb * cr - sb * sr
    sin = sb * cr + cb * sr

    def rotary(m, hd):
        m1 = m[:, hd * DK:hd * DK + HALF]
        m2 = m[:, hd * DK + HALF:(hd + 1) * DK]
        return jnp.concatenate([m1 * cos - m2 * sin, m1 * sin + m2 * cos], axis=1)

    ubuf[POOL_HALO:POOL_HALO + TILE, :] = proj(OFF_A_IN, OFF_A_GATE)
    q_all = proj(OFF_Q, OFF_K)
    qbs, qxs = [], []
    for hd in range(HEADS):
        qr = rotary(q_all, hd) * (DK ** -0.5)
        xi = xi_ref[hd]
        qbs.append(qr.astype(BF16))
        qxs.append((qr * jnp.concatenate([xi] * (DK // LANES), axis=1)).astype(BF16))
    k_all = proj(OFF_K, OFF_V)
    kbs, kzs = [], []
    for hd in range(HEADS):
        kr = rotary(k_all, hd)
        zeta = zeta_ref[hd]
        kbs.append(kr.astype(BF16))
        kzs.append((kr * jnp.concatenate([zeta] * (DK // LANES), axis=1)).astype(BF16))
    v_all = proj(OFF_V, OFF_RGATE)
    vbs = [v_all[:, hd * DV:(hd + 1) * DV].astype(BF16) for hd in range(HEADS)]
    a_gate = proj(OFF_A_GATE, OFF_Q)
    g_a = proj(OFF_GA, OFF_GB)

    pos1 = lax.broadcasted_iota(jnp.int32, (TILE, POOL_GROUP), 0) + (t * TILE + 1)
    mixed = []
    for g, w in enumerate(POOL_WINDOWS):
        cols = slice(g * POOL_GROUP, (g + 1) * POOL_GROUP)
        ext = ubuf[0:POOL_HALO + TILE, cols]
        win = ext
        span = 1
        while span < w:
            win = win + pltpu.roll(win, span, axis=0)
            span *= 2
        win = win[POOL_HALO:, :]
        u_g = ext[POOL_HALO:, :]
        cnt = jnp.minimum(pos1, w).astype(F32)
        pooled = win / cnt - u_g
        mixed.append(jnp.dot(pooled.astype(BF16), w_pool_s[g * POOL_GROUP:(g + 1) * POOL_GROUP, :],
                             preferred_element_type=F32))
    ubuf[0:POOL_HALO, :] = ubuf[TILE:TILE + POOL_HALO, :]
    ya_in = jnp.concatenate(mixed, axis=1) * pscale_ref[...] * _silu(a_gate)
    ya = jnp.dot(ya_in.astype(BF16), w_a_s[...], preferred_element_type=F32)
    merged = jax.nn.sigmoid(g_a) * ya

    gam = _gammas()
    ps, sts = [], []
    for hd in range(HEADS):
        scores = lax.dot_general(qbs[hd], kbs[hd], (((1,), (1,)), ((), ())),
                                 preferred_element_type=F32)
        st = state[hd]
        sts.append(st.astype(BF16))
        upd = lax.dot_general(kzs[hd], vbs[hd], (((0,), (0,)), ((), ())), preferred_element_type=F32)
        state[hd] = st * (gam[hd] ** TILE) + upd
        ps.append((scores * dmask_ref[hd]).astype(BF16))
    rg_all = proj(OFF_RGATE, OFF_GA)
    for hd in range(HEADS):
        lhs = jnp.concatenate([ps[hd], qxs[hd]], axis=1)
        rhs = jnp.concatenate([vbs[hd], sts[hd]], axis=0)
        o = jnp.dot(lhs, rhs, preferred_element_type=F32)
        on = o * lax.rsqrt(jnp.mean(o * o, axis=-1, keepdims=True) + EPS)
        rg = rg_all[:, hd * DV:(hd + 1) * DV]
        ybin[:, hd * DV:(hd + 1) * DV] = (on * _silu(rg)).astype(BF16)
    g_b = proj(OFF_GB, D_IN)
    yb = jnp.dot(ybin[...], w_b_s[...], preferred_element_type=F32)

    pend_m[...] = (merged + jax.nn.sigmoid(g_b) * yb).astype(BF16)
    pend_x[...] = x


def _decay_tables():
    log_g = np.log(np.asarray(_gammas(), np.float64))
    idx = np.arange(TILE)
    diff = idx[:, None] - idx[None, :]
    same = (idx[:, None] // CHUNK) == (idx[None, :] // CHUNK)
    dist = np.where(same, np.abs(diff), diff).astype(np.float64)
    keep = same | (diff > 0)
    dmask = np.where(keep[None], np.exp(log_g[:, None, None] * dist[None]), 0.0)
    xi = np.exp(log_g[:, None] * (idx[None, :] + 1.0))
    zeta = np.exp(log_g[:, None] * (TILE - 1.0 - idx[None, :]))
    rep = lambda a: np.ascontiguousarray(
        np.broadcast_to(a[:, :, None], (HEADS, TILE, LANES)), dtype=np.float32)
    return dmask.astype(np.float32), rep(xi), rep(zeta)


def _rope_tables(nt):
    inv_freq = 1.0 / (ROPE_BASE ** (jnp.arange(HALF, dtype=F32) / HALF))
    base = (jnp.arange(nt, dtype=jnp.int32) * TILE).astype(F32)[:, None] * inv_freq[None, :]
    rel = jnp.arange(TILE, dtype=jnp.int32).astype(F32)[:, None] * inv_freq[None, :]
    base = base.reshape(nt, 1, HALF)
    return jnp.cos(base), jnp.sin(base), jnp.cos(rel), jnp.sin(rel)


def _resident(shape):
    zeros = (0,) * len(shape)
    return pl.BlockSpec(shape, lambda s: zeros, pipeline_mode=pl.Buffered(1))


@jax.jit
def kernel(x, norm_in, w_in, w_pool, pool_scale, w_a, w_b, w_o, norm_out):
    batch, seq, d = x.shape
    assert d == D_MODEL and seq % TILE == 0 and TILE % CHUNK == 0
    nt = seq // TILE
    cosb, sinb, cosr, sinr = _rope_tables(nt)
    dmask, xi, zeta = _decay_tables()
    row = lambda v: v.reshape(1, -1).astype(F32)
    w_pool2d = w_pool.reshape(-1, POOL_GROUP)
    weights = (w_in, w_pool2d, w_a, w_b, w_o)
    num_tiles = batch * nt
    cur = lambda s: jnp.minimum(s, num_tiles - 1)
    in_spec = pl.BlockSpec((TILE, D_MODEL), lambda s: (cur(s), 0))
    out_spec = pl.BlockSpec((TILE, D_MODEL), lambda s: (jnp.maximum(s - 1, 0), 0))
    base_spec = pl.BlockSpec((1, 1, HALF), lambda s: (lax.rem(cur(s), nt), 0, 0))
    hbm_spec = pl.BlockSpec(memory_space=pl.ANY)
    out = pl.pallas_call(
        functools.partial(_block_kernel, tiles_per_seq=nt, num_tiles=num_tiles),
        grid=(num_tiles + 1,),
        in_specs=[
            in_spec, base_spec, base_spec, _resident((TILE, HALF)), _resident((TILE, HALF)),
            _resident((1, D_MODEL)), _resident((1, D_MODEL)), _resident((1, D_MODEL)),
            _resident(dmask.shape), _resident(xi.shape), _resident(zeta.shape),
        ] + [hbm_spec] * len(weights),
        out_specs=out_spec,
        out_shape=jax.ShapeDtypeStruct((batch * seq, D_MODEL), x.dtype),
        scratch_shapes=[pltpu.VMEM(w.shape, BF16) for w in weights] + [
            pltpu.VMEM((STAGE_SLOTS, STAGE_ROWS, STAGE_COLS), F32),
            pltpu.VMEM((w_pool2d.shape[0] // STAGE_ROWS, STAGE_ROWS, POOL_GROUP), F32),
            pltpu.SemaphoreType.DMA((STAGE_SLOTS,)),
            pltpu.SemaphoreType.DMA((w_pool2d.shape[0] // STAGE_ROWS,)),
            pltpu.VMEM((POOL_HALO + TILE, D_MODEL), F32),
            pltpu.VMEM((HEADS, DK, DV), F32),
            pltpu.VMEM((TILE, HEADS * DV), BF16),
            pltpu.VMEM((TILE, D_MODEL), BF16),
            pltpu.VMEM((TILE, D_MODEL), F32),
        ],
        compiler_params=pltpu.CompilerParams(
            dimension_semantics=("arbitrary",),
            vmem_limit_bytes=VMEM_LIMIT_BYTES),
        name="hybrid_block",
    )(x.reshape(batch * seq, D_MODEL), cosb, sinb, cosr, sinr,
      row(norm_in), row(norm_out), row(pool_scale), dmask, xi, zeta, *weights)
    return out.reshape(batch, seq, D_MODEL)
```

```python
import jax
import jax.numpy as jnp
import numpy as np
from jax import lax
from jax.experimental import pallas as pl
from jax.experimental.pallas import tpu as pltpu

D_MODEL = 1024
CHUNK = 64
POOL_WINDOWS = (2, 4, 8, 16)
POOL_GROUP = D_MODEL // len(POOL_WINDOWS)
POOL_HALO = max(POOL_WINDOWS)
HEADS = 4
DK = D_MODEL // HEADS
DV = 2 * D_MODEL // HEADS
HALF = DK // 2
LANES = 128
ROPE_BASE = 10000.0
EPS = 1e-6

OFF_A_IN, OFF_A_GATE, OFF_Q, OFF_K, OFF_V, OFF_RGATE, OFF_GA, OFF_GB, D_IN = (
    0, 1024, 2048, 3072, 4096, 6144, 8192, 9216, 10240)

TILE = 256
TILES_PER_STEP = 2
STAGE_ROWS, STAGE_COLS = 256, 1024
STAGE_SLOTS = 6
VMEM_LIMIT_BYTES = 58 * 1024 * 1024

F32 = jnp.float32
BF16 = jnp.bfloat16


def _gammas():
    return [1.0 - 2.0 ** (-5.0 - h) for h in range(HEADS)]


def _silu(v):
    return v * jax.nn.sigmoid(v)


def _weight_blocks(pairs, rows, cols):
    blocks = []
    for w_hbm, w_vmem in pairs:
        k, n = w_hbm.shape
        assert k % rows == 0 and n % cols == 0
        blocks += [(w_hbm, w_vmem, r0, c0) for r0 in range(0, k, rows) for c0 in range(0, n, cols)]
    return blocks


def _load_cast_weights(rings):
    plans = []
    for pairs, stage, sem in rings:
        nslot, rows, cols = stage.shape
        blocks = _weight_blocks(pairs, rows, cols)

        def chunk_copy(i, blocks=blocks, stage=stage, sem=sem, nslot=nslot, rows=rows, cols=cols):
            w_hbm, _, r0, c0 = blocks[i]
            return pltpu.make_async_copy(w_hbm.at[r0:r0 + rows, c0:c0 + cols],
                                         stage.at[i % nslot], sem.at[i % nslot])

        for i in range(min(nslot, len(blocks))):
            chunk_copy(i).start(priority=i % 2)
        plans.append((blocks, stage, nslot, rows, cols, chunk_copy))
    for blocks, stage, nslot, rows, cols, chunk_copy in plans:
        for i, (_, w_vmem, r0, c0) in enumerate(blocks):
            chunk_copy(i).wait()
            w_vmem[r0:r0 + rows, c0:c0 + cols] = stage[i % nslot].astype(BF16)
            if i + nslot < len(blocks):
                chunk_copy(i + nslot).start(priority=i % 2)


def _block_kernel(x_ref, cosb_ref, sinb_ref, cosr_ref, sinr_ref,
                  nin_ref, nout_ref, pscale_ref, dmask_ref, xi_ref, zeta_ref,
                  w_in_hbm, w_pool_hbm, w_a_hbm, w_b_hbm, w_o_hbm,
                  o_ref,
                  w_in_s, w_pool_s, w_a_s, w_b_s, w_o_s, stage, stage_pool, sem, sem_pool,
                  ubuf, state, ybin):
    step = pl.program_id(1)

    @pl.when((pl.program_id(0) == 0) & (step == 0))
    def _load_weights():
        wide = ((w_in_hbm, w_in_s), (w_a_hbm, w_a_s), (w_b_hbm, w_b_s), (w_o_hbm, w_o_s))
        _load_cast_weights([(wide, stage, sem), (((w_pool_hbm, w_pool_s),), stage_pool, sem_pool)])

    @pl.when(step == 0)
    def _reset_carries():
        ubuf[0:POOL_HALO, :] = jnp.zeros((POOL_HALO, D_MODEL), F32)
        state[...] = jnp.zeros(state.shape, F32)

    for sub in range(TILES_PER_STEP):
        rows = slice(sub * TILE, (sub + 1) * TILE)
        _tile(step * TILES_PER_STEP + sub, x_ref.at[rows], o_ref.at[rows], cosb_ref.at[sub], sinb_ref.at[sub],
              cosr_ref, sinr_ref, nin_ref, nout_ref, pscale_ref, dmask_ref, xi_ref, zeta_ref,
              w_in_s, w_pool_s, w_a_s, w_b_s, w_o_s, ubuf, state, ybin)


def _tile(t, x_ref, o_ref, cosb_ref, sinb_ref, cosr_ref, sinr_ref, nin_ref, nout_ref, pscale_ref,
          dmask_ref, xi_ref, zeta_ref, w_in_s, w_pool_s, w_a_s, w_b_s, w_o_s, ubuf, state, ybin):
    x = x_ref[...]
    h = x * lax.rsqrt(jnp.mean(x * x, axis=-1, keepdims=True) + EPS) * nin_ref[...]
    hb = h.astype(BF16)

    def proj(lo, hi):
        return jnp.dot(hb, w_in_s[:, lo:hi], preferred_element_type=F32)

    cb, sb = cosb_ref[...], sinb_ref[...]
    cr, sr = cosr_ref[...], sinr_ref[...]
    cos = cb * cr - sb * sr
    sin = sb * cr + cb * sr

    def rotary(m, hd):
        m1 = m[:, hd * DK:hd * DK + HALF]
        m2 = m[:, hd * DK + HALF:(hd + 1) * DK]
        return jnp.concatenate([m1 * cos - m2 * sin, m1 * sin + m2 * cos], axis=1)

    ubuf[POOL_HALO:POOL_HALO + TILE, :] = proj(OFF_A_IN, OFF_A_GATE)
    q_all = proj(OFF_Q, OFF_K)
    qbs, qxs = [], []
    for hd in range(HEADS):
        qr = rotary(q_all, hd) * (DK ** -0.5)
        xi = xi_ref[hd]
        qbs.append(qr.astype(BF16))
        qxs.append((qr * jnp.concatenate([xi] * (DK // LANES), axis=1)).astype(BF16))
    k_all = proj(OFF_K, OFF_V)
    kbs, kzs = [], []
    for hd in range(HEADS):
        kr = rotary(k_all, hd)
        zeta = zeta_ref[hd]
        kbs.append(kr.astype(BF16))
        kzs.append((kr * jnp.concatenate([zeta] * (DK // LANES), axis=1)).astype(BF16))
    v_all = proj(OFF_V, OFF_RGATE)
    vbs = [v_all[:, hd * DV:(hd + 1) * DV].astype(BF16) for hd in range(HEADS)]
    a_gate = proj(OFF_A_GATE, OFF_Q)
    g_a = proj(OFF_GA, OFF_GB)

    pos1 = lax.broadcasted_iota(jnp.int32, (TILE, POOL_GROUP), 0) + (t * TILE + 1)
    mixed = []
    for g, w in enumerate(POOL_WINDOWS):
        cols = slice(g * POOL_GROUP, (g + 1) * POOL_GROUP)
        ext = ubuf[0:POOL_HALO + TILE, cols]
        win = ext
        span = 1
        while span < w:
            win = win + pltpu.roll(win, span, axis=0)
            span *= 2
        win = win[POOL_HALO:, :]
        u_g = ext[POOL_HALO:, :]
        cnt = jnp.minimum(pos1, w).astype(F32)
        pooled = win / cnt - u_g
        mixed.append(jnp.dot(pooled.astype(BF16), w_pool_s[g * POOL_GROUP:(g + 1) * POOL_GROUP, :],
                             preferred_element_type=F32))
    ubuf[0:POOL_HALO, :] = ubuf[TILE:TILE + POOL_HALO, :]
    ya_in = jnp.concatenate(mixed, axis=1) * pscale_ref[...] * _silu(a_gate)
    ya = jnp.dot(ya_in.astype(BF16), w_a_s[...], preferred_element_type=F32)
    merged = jax.nn.sigmoid(g_a) * ya

    gam = _gammas()
    ps, sts = [], []
    for hd in range(HEADS):
        scores = lax.dot_general(qbs[hd], kbs[hd], (((1,), (1,)), ((), ())),
                                 preferred_element_type=F32)
        st = state[hd]
        sts.append(st.astype(BF16))
        upd = lax.dot_general(kzs[hd], vbs[hd], (((0,), (0,)), ((), ())), preferred_element_type=F32)
        state[hd] = st * (gam[hd] ** TILE) + upd
        ps.append((scores * dmask_ref[hd]).astype(BF16))
    rg_all = proj(OFF_RGATE, OFF_GA)
    for hd in range(HEADS):
        lhs = jnp.concatenate([ps[hd], qxs[hd]], axis=1)
        rhs = jnp.concatenate([vbs[hd], sts[hd]], axis=0)
        o = jnp.dot(lhs, rhs, preferred_element_type=F32)
        on = o * lax.rsqrt(jnp.mean(o * o, axis=-1, keepdims=True) + EPS)
        rg = rg_all[:, hd * DV:(hd + 1) * DV]
        ybin[:, hd * DV:(hd + 1) * DV] = (on * _silu(rg)).astype(BF16)
    g_b = proj(OFF_GB, D_IN)
    yb = jnp.dot(ybin[...], w_b_s[...], preferred_element_type=F32)

    merged = merged + jax.nn.sigmoid(g_b) * yb
    y = x + jnp.dot(merged.astype(BF16), w_o_s[...], preferred_element_type=F32)
    o_ref[...] = y * lax.rsqrt(jnp.mean(y * y, axis=-1, keepdims=True) + EPS) * nout_ref[...]


def _decay_tables():
    log_g = np.log(np.asarray(_gammas(), np.float64))
    idx = np.arange(TILE)
    diff = idx[:, None] - idx[None, :]
    same = (idx[:, None] // CHUNK) == (idx[None, :] // CHUNK)
    dist = np.where(same, np.abs(diff), diff).astype(np.float64)
    keep = same | (diff > 0)
    dmask = np.where(keep[None], np.exp(log_g[:, None, None] * dist[None]), 0.0)
    xi = np.exp(log_g[:, None] * (idx[None, :] + 1.0))
    zeta = np.exp(log_g[:, None] * (TILE - 1.0 - idx[None, :]))
    rep = lambda a: np.ascontiguousarray(
        np.broadcast_to(a[:, :, None], (HEADS, TILE, LANES)), dtype=np.float32)
    return dmask.astype(np.float32), rep(xi), rep(zeta)


def _rope_tables(nt):
    inv_freq = 1.0 / (ROPE_BASE ** (jnp.arange(HALF, dtype=F32) / HALF))
    base = (jnp.arange(nt, dtype=jnp.int32) * TILE).astype(F32)[:, None] * inv_freq[None, :]
    rel = jnp.arange(TILE, dtype=jnp.int32).astype(F32)[:, None] * inv_freq[None, :]
    base = base.reshape(nt, 1, HALF)
    return jnp.cos(base), jnp.sin(base), jnp.cos(rel), jnp.sin(rel)


def _resident(shape):
    zeros = (0,) * len(shape)
    return pl.BlockSpec(shape, lambda b, t: zeros, pipeline_mode=pl.Buffered(1))


@jax.jit
def kernel(x, norm_in, w_in, w_pool, pool_scale, w_a, w_b, w_o, norm_out):
    batch, seq, d = x.shape
    step_rows = TILE * TILES_PER_STEP
    assert d == D_MODEL and seq % step_rows == 0 and TILE % CHUNK == 0
    nt = seq // TILE
    steps = seq // step_rows
    cosb, sinb, cosr, sinr = _rope_tables(nt)
    dmask, xi, zeta = _decay_tables()
    row = lambda v: v.reshape(1, -1).astype(F32)
    w_pool2d = w_pool.reshape(-1, POOL_GROUP)
    weights = (w_in, w_pool2d, w_a, w_b, w_o)
    tok_spec = pl.BlockSpec((step_rows, D_MODEL), lambda b, t: (b * steps + t, 0))
    base_spec = pl.BlockSpec((TILES_PER_STEP, 1, HALF), lambda b, t: (t, 0, 0))
    hbm_spec = pl.BlockSpec(memory_space=pl.ANY)
    out = pl.pallas_call(
        _block_kernel,
        grid=(batch, steps),
        in_specs=[
            tok_spec, base_spec, base_spec, _resident((TILE, HALF)), _resident((TILE, HALF)),
            _resident((1, D_MODEL)), _resident((1, D_MODEL)), _resident((1, D_MODEL)),
            _resident(dmask.shape), _resident(xi.shape), _resident(zeta.shape),
        ] + [hbm_spec] * len(weights),
        out_specs=tok_spec,
        out_shape=jax.ShapeDtypeStruct((batch * seq, D_MODEL), x.dtype),
        scratch_shapes=[pltpu.VMEM(w.shape, BF16) for w in weights] + [
            pltpu.VMEM((STAGE_SLOTS, STAGE_ROWS, STAGE_COLS), F32),
            pltpu.VMEM((w_pool2d.shape[0] // STAGE_ROWS, STAGE_ROWS, POOL_GROUP), F32),
            pltpu.SemaphoreType.DMA((STAGE_SLOTS,)),
            pltpu.SemaphoreType.DMA((w_pool2d.shape[0] // STAGE_ROWS,)),
            pltpu.VMEM((POOL_HALO + TILE, D_MODEL), F32),
            pltpu.VMEM((HEADS, DK, DV), F32),
            pltpu.VMEM((TILE, HEADS * DV), BF16),
        ],
        compiler_params=pltpu.CompilerParams(
            dimension_semantics=("arbitrary", "arbitrary"),
            vmem_limit_bytes=VMEM_LIMIT_BYTES),
        name="hybrid_block",
    )(x.reshape(batch * seq, D_MODEL), cosb, sinb, cosr, sinr,
      row(norm_in), row(norm_out), row(pool_scale), dmask, xi, zeta, *weights)
    return out.reshape(batch, seq, D_MODEL)
```

```python
import jax
import jax.numpy as jnp
import numpy as np
from jax import lax
from jax.experimental import pallas as pl
from jax.experimental.pallas import tpu as pltpu

D_MODEL = 1024
CHUNK = 64
POOL_WINDOWS = (2, 4, 8, 16)
POOL_GROUP = D_MODEL // len(POOL_WINDOWS)
POOL_HALO = max(POOL_WINDOWS)
HEADS = 4
DK = D_MODEL // HEADS
DV = 2 * D_MODEL // HEADS
HALF = DK // 2
LANES = 128
ROPE_BASE = 10000.0
EPS = 1e-6

OFF_A_IN, OFF_A_GATE, OFF_Q, OFF_K, OFF_V, OFF_RGATE, OFF_GA, OFF_GB, D_IN = (
    0, 1024, 2048, 3072, 4096, 6144, 8192, 9216, 10240)

TILE = 256
TILES_PER_STEP = 2
STAGE_ROWS, STAGE_COLS = 256, 1024
STAGE_SLOTS = 8
VMEM_LIMIT_BYTES = 60 * 1024 * 1024

F32 = jnp.float32
BF16 = jnp.bfloat16


def _gammas():
    return [1.0 - 2.0 ** (-5.0 - h) for h in range(HEADS)]


def _silu(v):
    return v * jax.nn.sigmoid(v)


def _weight_blocks(pairs, rows, cols):
    blocks = []
    for w_hbm, w_vmem in pairs:
        k, n = w_hbm.shape
        assert k % rows == 0 and n % cols == 0
        blocks += [(w_hbm, w_vmem, r0, c0) for r0 in range(0, k, rows) for c0 in range(0, n, cols)]
    return blocks


def _load_cast_weights(rings):
    plans = []
    for pairs, stage, sem in rings:
        nslot, rows, cols = stage.shape
        blocks = _weight_blocks(pairs, rows, cols)

        def chunk_copy(i, blocks=blocks, stage=stage, sem=sem, nslot=nslot, rows=rows, cols=cols):
            w_hbm, _, r0, c0 = blocks[i]
            return pltpu.make_async_copy(w_hbm.at[r0:r0 + rows, c0:c0 + cols],
                                         stage.at[i % nslot], sem.at[i % nslot])

        for i in range(min(nslot, len(blocks))):
            chunk_copy(i).start(priority=i % 2)
        plans.append((blocks, stage, nslot, rows, cols, chunk_copy))
    for blocks, stage, nslot, rows, cols, chunk_copy in plans:
        for i, (_, w_vmem, r0, c0) in enumerate(blocks):
            chunk_copy(i).wait()
            w_vmem[r0:r0 + rows, c0:c0 + cols] = stage[i % nslot].astype(BF16)
            if i + nslot < len(blocks):
                chunk_copy(i + nslot).start(priority=i % 2)


def _block_kernel(x_ref, cosb_ref, sinb_ref, cosr_ref, sinr_ref,
                  nin_ref, nout_ref, pscale_ref, dmask_ref, xi_ref, zeta_ref,
                  w_in_hbm, w_pool_hbm, w_a_hbm, w_b_hbm, w_o_hbm,
                  o_ref,
                  w_in_s, w_pool_s, w_a_s, w_b_s, w_o_s, stage, stage_pool, sem, sem_pool,
                  ubuf, state, ybin):
    step = pl.program_id(1)

    @pl.when((pl.program_id(0) == 0) & (step == 0))
    def _load_weights():
        wide = ((w_in_hbm, w_in_s), (w_a_hbm, w_a_s), (w_b_hbm, w_b_s), (w_o_hbm, w_o_s))
        _load_cast_weights([(wide, stage, sem), (((w_pool_hbm, w_pool_s),), stage_pool, sem_pool)])

    @pl.when(step == 0)
    def _reset_carries():
        ubuf[0:POOL_HALO, :] = jnp.zeros((POOL_HALO, D_MODEL), F32)
        state[...] = jnp.zeros(state.shape, F32)

    for sub in range(TILES_PER_STEP):
        rows = slice(sub * TILE, (sub + 1) * TILE)
        _tile(step * TILES_PER_STEP + sub, x_ref.at[rows], o_ref.at[rows], cosb_ref.at[sub], sinb_ref.at[sub],
              cosr_ref, sinr_ref, nin_ref, nout_ref, pscale_ref, dmask_ref, xi_ref, zeta_ref,
              w_in_s, w_pool_s, w_a_s, w_b_s, w_o_s, ubuf, state, ybin)


def _tile(t, x_ref, o_ref, cosb_ref, sinb_ref, cosr_ref, sinr_ref, nin_ref, nout_ref, pscale_ref,
          dmask_ref, xi_ref, zeta_ref, w_in_s, w_pool_s, w_a_s, w_b_s, w_o_s, ubuf, state, ybin):
    x = x_ref[...]
    h = x * lax.rsqrt(jnp.mean(x * x, axis=-1, keepdims=True) + EPS) * nin_ref[...]
    hb = h.astype(BF16)

    def proj(lo, hi):
        return jnp.dot(hb, w_in_s[:, lo:hi], preferred_element_type=F32)

    cb, sb = cosb_ref[...], sinb_ref[...]
    cr, sr = cosr_ref[...], sinr_ref[...]
    cos = cb * cr - sb * sr
    sin = sb * cr + cb * sr

    def rotary(m, hd):
        m1 = m[:, hd * DK:hd * DK + HALF]
        m2 = m[:, hd * DK + HALF:(hd + 1) * DK]
        return jnp.concatenate([m1 * cos - m2 * sin, m1 * sin + m2 * cos], axis=1)

    ubuf[POOL_HALO:POOL_HALO + TILE, :] = proj(OFF_A_IN, OFF_A_GATE)
    q_all = proj(OFF_Q, OFF_K)
    qbs, qxs = [], []
    for hd in range(HEADS):
        qr = rotary(q_all, hd) * (DK ** -0.5)
        xi = xi_ref[hd]
        qbs.append(qr.astype(BF16))
        qxs.append((qr * jnp.concatenate([xi] * (DK // LANES), axis=1)).astype(BF16))
    k_all = proj(OFF_K, OFF_V)
    kbs, kzs = [], []
    for hd in range(HEADS):
        kr = rotary(k_all, hd)
        zeta = zeta_ref[hd]
        kbs.append(kr.astype(BF16))
        kzs.append((kr * jnp.concatenate([zeta] * (DK // LANES), axis=1)).astype(BF16))
    v_all = proj(OFF_V, OFF_RGATE)
    vbs = [v_all[:, hd * DV:(hd + 1) * DV].astype(BF16) for hd in range(HEADS)]
    a_gate = proj(OFF_A_GATE, OFF_Q)
    g_a = proj(OFF_GA, OFF_GB)

    pos1 = lax.broadcasted_iota(jnp.int32, (TILE, POOL_GROUP), 0) + (t * TILE + 1)
    mixed = []
    for g, w in enumerate(POOL_WINDOWS):
        cols = slice(g * POOL_GROUP, (g + 1) * POOL_GROUP)
        ext = ubuf[0:POOL_HALO + TILE, cols]
        win = ext
        span = 1
        while span < w:
            win = win + pltpu.roll(win, span, axis=0)
            span *= 2
        win = win[POOL_HALO:, :]
        u_g = ext[POOL_HALO:, :]
        cnt = jnp.minimum(pos1, w).astype(F32)
        pooled = win / cnt - u_g
        mixed.append(jnp.dot(pooled.astype(BF16), w_pool_s[g * POOL_GROUP:(g + 1) * POOL_GROUP, :],
                             preferred_element_type=F32))
    ubuf[0:POOL_HALO, :] = ubuf[TILE:TILE + POOL_HALO, :]
    ya_in = jnp.concatenate(mixed, axis=1) * pscale_ref[...] * _silu(a_gate)
    ya = jnp.dot(ya_in.astype(BF16), w_a_s[...], preferred_element_type=F32)
    merged = jax.nn.sigmoid(g_a) * ya

    gam = _gammas()
    ps, sts = [], []
    for hd in range(HEADS):
        scores = lax.dot_general(qbs[hd], kbs[hd], (((1,), (1,)), ((), ())),
                                 preferred_element_type=F32)
        st = state[hd]
        sts.append(st.astype(BF16))
        upd = lax.dot_general(kzs[hd], vbs[hd], (((0,), (0,)), ((), ())), preferred_element_type=F32)
        state[hd] = st * (gam[hd] ** TILE) + upd
        ps.append((scores * dmask_ref[hd]).astype(BF16))
    rg_all = proj(OFF_RGATE, OFF_GA)
    for hd in range(HEADS):
        lhs = jnp.concatenate([ps[hd], qxs[hd]], axis=1)
        rhs = jnp.concatenate([vbs[hd], sts[hd]], axis=0)
        o = jnp.dot(lhs, rhs, preferred_element_type=F32)
        on = o * lax.rsqrt(jnp.mean(o * o, axis=-1, keepdims=True) + EPS)
        rg = rg_all[:, hd * DV:(hd + 1) * DV]
        ybin[:, hd * DV:(hd + 1) * DV] = (on * _silu(rg)).astype(BF16)
    g_b = proj(OFF_GB, D_IN)
    yb = jnp.dot(ybin[...], w_b_s[...], preferred_element_type=F32)

    merged = merged + jax.nn.sigmoid(g_b) * yb
    y = x + jnp.dot(merged.astype(BF16), w_o_s[...], preferred_element_type=F32)
    o_ref[...] = y * lax.rsqrt(jnp.mean(y * y, axis=-1, keepdims=True) + EPS) * nout_ref[...]


def _decay_tables():
    log_g = np.log(np.asarray(_gammas(), np.float64))
    idx = np.arange(TILE)
    diff = idx[:, None] - idx[None, :]
    same = (idx[:, None] // CHUNK) == (idx[None, :] // CHUNK)
    dist = np.where(same, np.abs(diff), diff).astype(np.float64)
    keep = same | (diff > 0)
    dmask = np.where(keep[None], np.exp(log_g[:, None, None] * dist[None]), 0.0)
    xi = np.exp(log_g[:, None] * (idx[None, :] + 1.0))
    zeta = np.exp(log_g[:, None] * (TILE - 1.0 - idx[None, :]))
    rep = lambda a: np.ascontiguousarray(
        np.broadcast_to(a[:, :, None], (HEADS, TILE, LANES)), dtype=np.float32)
    return dmask.astype(np.float32), rep(xi), rep(zeta)


def _rope_tables(nt):
    inv_freq = 1.0 / (ROPE_BASE ** (jnp.arange(HALF, dtype=F32) / HALF))
    base = (jnp.arange(nt, dtype=jnp.int32) * TILE).astype(F32)[:, None] * inv_freq[None, :]
    rel = jnp.arange(TILE, dtype=jnp.int32).astype(F32)[:, None] * inv_freq[None, :]
    base = base.reshape(nt, 1, HALF)
    return jnp.cos(base), jnp.sin(base), jnp.cos(rel), jnp.sin(rel)


def _resident(shape):
    zeros = (0,) * len(shape)
    return pl.BlockSpec(shape, lambda b, t: zeros, pipeline_mode=pl.Buffered(1))


@jax.jit
def kernel(x, norm_in, w_in, w_pool, pool_scale, w_a, w_b, w_o, norm_out):
    batch, seq, d = x.shape
    step_rows = TILE * TILES_PER_STEP
    assert d == D_MODEL and seq % step_rows == 0 and TILE % CHUNK == 0
    nt = seq // TILE
    steps = seq // step_rows
    cosb, sinb, cosr, sinr = _rope_tables(nt)
    dmask, xi, zeta = _decay_tables()
    row = lambda v: v.reshape(1, -1).astype(F32)
    w_pool2d = w_pool.reshape(-1, POOL_GROUP)
    weights = (w_in, w_pool2d, w_a, w_b, w_o)
    tok_spec = pl.BlockSpec((step_rows, D_MODEL), lambda b, t: (b * steps + t, 0))
    base_spec = pl.BlockSpec((TILES_PER_STEP, 1, HALF), lambda b, t: (t, 0, 0))
    hbm_spec = pl.BlockSpec(memory_space=pl.ANY)
    out = pl.pallas_call(
        _block_kernel,
        grid=(batch, steps),
        in_specs=[
            tok_spec, base_spec, base_spec, _resident((TILE, HALF)), _resident((TILE, HALF)),
            _resident((1, D_MODEL)), _resident((1, D_MODEL)), _resident((1, D_MODEL)),
            _resident(dmask.shape), _resident(xi.shape), _resident(zeta.shape),
        ] + [hbm_spec] * len(weights),
        out_specs=tok_spec,
        out_shape=jax.ShapeDtypeStruct((batch * seq, D_MODEL), x.dtype),
        scratch_shapes=[pltpu.VMEM(w.shape, BF16) for w in weights] + [
            pltpu.VMEM((STAGE_SLOTS, STAGE_ROWS, STAGE_COLS), F32),
            pltpu.VMEM((w_pool2d.shape[0] // STAGE_ROWS, STAGE_ROWS, POOL_GROUP), F32),
            pltpu.SemaphoreType.DMA((STAGE_SLOTS,)),
            pltpu.SemaphoreType.DMA((w_pool2d.shape[0] // STAGE_ROWS,)),
            pltpu.VMEM((POOL_HALO + TILE, D_MODEL), F32),
            pltpu.VMEM((HEADS, DK, DV), F32),
            pltpu.VMEM((TILE, HEADS * DV), BF16),
        ],
        compiler_params=pltpu.CompilerParams(
            dimension_semantics=("arbitrary", "arbitrary"),
            vmem_limit_bytes=VMEM_LIMIT_BYTES),
        name="hybrid_block",
    )(x.reshape(batch * seq, D_MODEL), cosb, sinb, cosr, sinr,
      row(norm_in), row(norm_out), row(pool_scale), dmask, xi, zeta, *weights)
    return out.reshape(batch, seq, D_MODEL)
```

```python
import jax
import jax.numpy as jnp
import numpy as np
from jax import lax
from jax.experimental import pallas as pl
from jax.experimental.pallas import tpu as pltpu

D_MODEL = 1024
CHUNK = 64
POOL_WINDOWS = (2, 4, 8, 16)
POOL_GROUP = D_MODEL // len(POOL_WINDOWS)
POOL_HALO = max(POOL_WINDOWS)
HEADS = 4
DK = D_MODEL // HEADS
DV = 2 * D_MODEL // HEADS
HALF = DK // 2
LANES = 128
ROPE_BASE = 10000.0
EPS = 1e-6

OFF_A_IN, OFF_A_GATE, OFF_Q, OFF_K, OFF_V, OFF_RGATE, OFF_GA, OFF_GB, D_IN = (
    0, 1024, 2048, 3072, 4096, 6144, 8192, 9216, 10240)

TILE = 256
TILES_PER_STEP = 2
STAGE_ROWS = 256
STAGE_SLOTS = 3
STAGE_IN_ROWS = 32
STAGE_IN_SLOTS = 4
VMEM_LIMIT_BYTES = 60 * 1024 * 1024

F32 = jnp.float32
BF16 = jnp.bfloat16


def _gammas():
    return [1.0 - 2.0 ** (-5.0 - h) for h in range(HEADS)]


def _silu(v):
    return v * jax.nn.sigmoid(v)


def _weight_blocks(pairs, rows, cols):
    blocks = []
    for w_hbm, w_vmem in pairs:
        k, n = w_hbm.shape
        assert k % rows == 0 and n % cols == 0
        blocks += [(w_hbm, w_vmem, r0, c0) for r0 in range(0, k, rows) for c0 in range(0, n, cols)]
    return blocks


def _load_cast_weights(rings):
    plans = []
    for pairs, stage, sem in rings:
        nslot, rows, cols = stage.shape
        blocks = _weight_blocks(pairs, rows, cols)

        def chunk_copy(i, blocks=blocks, stage=stage, sem=sem, nslot=nslot, rows=rows, cols=cols):
            w_hbm, _, r0, c0 = blocks[i]
            return pltpu.make_async_copy(w_hbm.at[r0:r0 + rows, c0:c0 + cols],
                                         stage.at[i % nslot], sem.at[i % nslot])

        for i in range(min(nslot, len(blocks))):
            chunk_copy(i).start(priority=i % 2)
        plans.append((blocks, stage, nslot, rows, cols, chunk_copy))
    for blocks, stage, nslot, rows, cols, chunk_copy in plans:
        for i, (_, w_vmem, r0, c0) in enumerate(blocks):
            chunk_copy(i).wait()
            w_vmem[r0:r0 + rows, c0:c0 + cols] = stage[i % nslot].astype(BF16)
            if i + nslot < len(blocks):
                chunk_copy(i + nslot).start(priority=i % 2)


def _block_kernel(x_ref, cosb_ref, sinb_ref, cosr_ref, sinr_ref,
                  nin_ref, nout_ref, pscale_ref, dmask_ref, xi_ref, zeta_ref,
                  w_in_hbm, w_pool_hbm, w_a_hbm, w_b_hbm, w_o_hbm,
                  o_ref,
                  w_in_s, w_pool_s, w_a_s, w_b_s, w_o_s,
                  stage_in, stage, stage_pool, sem_in, sem, sem_pool,
                  ubuf, state, ybin):
    step = pl.program_id(1)

    @pl.when((pl.program_id(0) == 0) & (step == 0))
    def _load_weights():
        square = ((w_a_hbm, w_a_s), (w_b_hbm, w_b_s), (w_o_hbm, w_o_s))
        _load_cast_weights([(((w_in_hbm, w_in_s),), stage_in, sem_in), (square, stage, sem),
                            (((w_pool_hbm, w_pool_s),), stage_pool, sem_pool)])

    @pl.when(step == 0)
    def _reset_carries():
        ubuf[0:POOL_HALO, :] = jnp.zeros((POOL_HALO, D_MODEL), F32)
        state[...] = jnp.zeros(state.shape, F32)

    for sub in range(TILES_PER_STEP):
        rows = slice(sub * TILE, (sub + 1) * TILE)
        _tile(step * TILES_PER_STEP + sub, x_ref.at[rows], o_ref.at[rows], cosb_ref.at[sub], sinb_ref.at[sub],
              cosr_ref, sinr_ref, nin_ref, nout_ref, pscale_ref, dmask_ref, xi_ref, zeta_ref,
              w_in_s, w_pool_s, w_a_s, w_b_s, w_o_s, ubuf, state, ybin)


def _tile(t, x_ref, o_ref, cosb_ref, sinb_ref, cosr_ref, sinr_ref, nin_ref, nout_ref, pscale_ref,
          dmask_ref, xi_ref, zeta_ref, w_in_s, w_pool_s, w_a_s, w_b_s, w_o_s, ubuf, state, ybin):
    x = x_ref[...]
    h = x * lax.rsqrt(jnp.mean(x * x, axis=-1, keepdims=True) + EPS) * nin_ref[...]
    hb = h.astype(BF16)

    def proj(lo, hi):
        return jnp.dot(hb, w_in_s[:, lo:hi], preferred_element_type=F32)

    cb, sb = cosb_ref[...], sinb_ref[...]
    cr, sr = cosr_ref[...], sinr_ref[...]
    cos = cb * cr - sb * sr
    sin = sb * cr + cb * sr

    def rotary(m, hd):
        m1 = m[:, hd * DK:hd * DK + HALF]
        m2 = m[:, hd * DK + HALF:(hd + 1) * DK]
        return jnp.concatenate([m1 * cos - m2 * sin, m1 * sin + m2 * cos], axis=1)

    ubuf[POOL_HALO:POOL_HALO + TILE, :] = proj(OFF_A_IN, OFF_A_GATE)
    q_all = proj(OFF_Q, OFF_K)
    qbs, qxs = [], []
    for hd in range(HEADS):
        qr = rotary(q_all, hd) * (DK ** -0.5)
        xi = xi_ref[hd]
        qbs.append(qr.astype(BF16))
        qxs.append((qr * jnp.concatenate([xi] * (DK // LANES), axis=1)).astype(BF16))
    k_all = proj(OFF_K, OFF_V)
    kbs, kzs = [], []
    for hd in range(HEADS):
        kr = rotary(k_all, hd)
        zeta = zeta_ref[hd]
        kbs.append(kr.astype(BF16))
        kzs.append((kr * jnp.concatenate([zeta] * (DK // LANES), axis=1)).astype(BF16))
    v_all = proj(OFF_V, OFF_RGATE)
    vbs = [v_all[:, hd * DV:(hd + 1) * DV].astype(BF16) for hd in range(HEADS)]
    a_gate = proj(OFF_A_GATE, OFF_Q)
    g_a = proj(OFF_GA, OFF_GB)

    pos1 = lax.broadcasted_iota(jnp.int32, (TILE, POOL_GROUP), 0) + (t * TILE + 1)
    mixed = []
    for g, w in enumerate(POOL_WINDOWS):
        cols = slice(g * POOL_GROUP, (g + 1) * POOL_GROUP)
        ext = ubuf[0:POOL_HALO + TILE, cols]
        win = ext
        span = 1
        while span < w:
            win = win + pltpu.roll(win, span, axis=0)
            span *= 2
        win = win[POOL_HALO:, :]
        u_g = ext[POOL_HALO:, :]
        cnt = jnp.minimum(pos1, w).astype(F32)
        pooled = win / cnt - u_g
        mixed.append(jnp.dot(pooled.astype(BF16), w_pool_s[g * POOL_GROUP:(g + 1) * POOL_GROUP, :],
                             preferred_element_type=F32))
    ubuf[0:POOL_HALO, :] = ubuf[TILE:TILE + POOL_HALO, :]
    ya_in = jnp.concatenate(mixed, axis=1) * pscale_ref[...] * _silu(a_gate)
    ya = jnp.dot(ya_in.astype(BF16), w_a_s[...], preferred_element_type=F32)
    merged = jax.nn.sigmoid(g_a) * ya

    gam = _gammas()
    ps, sts = [], []
    for hd in range(HEADS):
        scores = lax.dot_general(qbs[hd], kbs[hd], (((1,), (1,)), ((), ())),
                                 preferred_element_type=F32)
        st = state[hd]
        sts.append(st.astype(BF16))
        upd = lax.dot_general(kzs[hd], vbs[hd], (((0,), (0,)), ((), ())), preferred_element_type=F32)
        state[hd] = st * (gam[hd] ** TILE) + upd
        ps.append((scores * dmask_ref[hd]).astype(BF16))
    rg_all = proj(OFF_RGATE, OFF_GA)
    for hd in range(HEADS):
        lhs = jnp.concatenate([ps[hd], qxs[hd]], axis=1)
        rhs = jnp.concatenate([vbs[hd], sts[hd]], axis=0)
        o = jnp.dot(lhs, rhs, preferred_element_type=F32)
        on = o * lax.rsqrt(jnp.mean(o * o, axis=-1, keepdims=True) + EPS)
        rg = rg_all[:, hd * DV:(hd + 1) * DV]
        ybin[:, hd * DV:(hd + 1) * DV] = (on * _silu(rg)).astype(BF16)
    g_b = proj(OFF_GB, D_IN)
    yb = jnp.dot(ybin[...], w_b_s[...], preferred_element_type=F32)

    merged = merged + jax.nn.sigmoid(g_b) * yb
    y = x + jnp.dot(merged.astype(BF16), w_o_s[...], preferred_element_type=F32)
    o_ref[...] = y * lax.rsqrt(jnp.mean(y * y, axis=-1, keepdims=True) + EPS) * nout_ref[...]


def _decay_tables():
    log_g = np.log(np.asarray(_gammas(), np.float64))
    idx = np.arange(TILE)
    diff = idx[:, None] - idx[None, :]
    same = (idx[:, None] // CHUNK) == (idx[None, :] // CHUNK)
    dist = np.where(same, np.abs(diff), diff).astype(np.float64)
    keep = same | (diff > 0)
    dmask = np.where(keep[None], np.exp(log_g[:, None, None] * dist[None]), 0.0)
    xi = np.exp(log_g[:, None] * (idx[None, :] + 1.0))
    zeta = np.exp(log_g[:, None] * (TILE - 1.0 - idx[None, :]))
    rep = lambda a: np.ascontiguousarray(
        np.broadcast_to(a[:, :, None], (HEADS, TILE, LANES)), dtype=np.float32)
    return dmask.astype(np.float32), rep(xi), rep(zeta)


def _rope_tables(nt):
    inv_freq = 1.0 / (ROPE_BASE ** (jnp.arange(HALF, dtype=F32) / HALF))
    base = (jnp.arange(nt, dtype=jnp.int32) * TILE).astype(F32)[:, None] * inv_freq[None, :]
    rel = jnp.arange(TILE, dtype=jnp.int32).astype(F32)[:, None] * inv_freq[None, :]
    base = base.reshape(nt, 1, HALF)
    return jnp.cos(base), jnp.sin(base), jnp.cos(rel), jnp.sin(rel)


def _resident(shape):
    zeros = (0,) * len(shape)
    return pl.BlockSpec(shape, lambda b, t: zeros, pipeline_mode=pl.Buffered(1))


@jax.jit
def kernel(x, norm_in, w_in, w_pool, pool_scale, w_a, w_b, w_o, norm_out):
    batch, seq, d = x.shape
    step_rows = TILE * TILES_PER_STEP
    assert d == D_MODEL and seq % step_rows == 0 and TILE % CHUNK == 0
    nt = seq // TILE
    steps = seq // step_rows
    cosb, sinb, cosr, sinr = _rope_tables(nt)
    dmask, xi, zeta = _decay_tables()
    row = lambda v: v.reshape(1, -1).astype(F32)
    w_pool2d = w_pool.reshape(-1, POOL_GROUP)
    weights = (w_in, w_pool2d, w_a, w_b, w_o)
    tok_spec = pl.BlockSpec((step_rows, D_MODEL), lambda b, t: (b * steps + t, 0))
    base_spec = pl.BlockSpec((TILES_PER_STEP, 1, HALF), lambda b, t: (t, 0, 0))
    hbm_spec = pl.BlockSpec(memory_space=pl.ANY)
    out = pl.pallas_call(
        _block_kernel,
        grid=(batch, steps),
        in_specs=[
            tok_spec, base_spec, base_spec, _resident((TILE, HALF)), _resident((TILE, HALF)),
            _resident((1, D_MODEL)), _resident((1, D_MODEL)), _resident((1, D_MODEL)),
            _resident(dmask.shape), _resident(xi.shape), _resident(zeta.shape),
        ] + [hbm_spec] * len(weights),
        out_specs=tok_spec,
        out_shape=jax.ShapeDtypeStruct((batch * seq, D_MODEL), x.dtype),
        scratch_shapes=[pltpu.VMEM(w.shape, BF16) for w in weights] + [
            pltpu.VMEM((STAGE_IN_SLOTS, STAGE_IN_ROWS, D_IN), F32),
            pltpu.VMEM((STAGE_SLOTS, STAGE_ROWS, D_MODEL), F32),
            pltpu.VMEM((w_pool2d.shape[0] // STAGE_ROWS, STAGE_ROWS, POOL_GROUP), F32),
            pltpu.SemaphoreType.DMA((STAGE_IN_SLOTS,)),
            pltpu.SemaphoreType.DMA((STAGE_SLOTS,)),
            pltpu.SemaphoreType.DMA((w_pool2d.shape[0] // STAGE_ROWS,)),
            pltpu.VMEM((POOL_HALO + TILE, D_MODEL), F32),
            pltpu.VMEM((HEADS, DK, DV), F32),
            pltpu.VMEM((TILE, HEADS * DV), BF16),
        ],
        compiler_params=pltpu.CompilerParams(
            dimension_semantics=("arbitrary", "arbitrary"),
            vmem_limit_bytes=VMEM_LIMIT_BYTES),
        name="hybrid_block",
    )(x.reshape(batch * seq, D_MODEL), cosb, sinb, cosr, sinr,
      row(norm_in), row(norm_out), row(pool_scale), dmask, xi, zeta, *weights)
    return out.reshape(batch, seq, D_MODEL)
```

```python
import jax
import jax.numpy as jnp
import numpy as np
from jax import lax
from jax.experimental import pallas as pl
from jax.experimental.pallas import tpu as pltpu

D_MODEL = 1024
CHUNK = 64
POOL_WINDOWS = (2, 4, 8, 16)
POOL_GROUP = D_MODEL // len(POOL_WINDOWS)
POOL_HALO = max(POOL_WINDOWS)
HEADS = 4
DK = D_MODEL // HEADS
DV = 2 * D_MODEL // HEADS
HALF = DK // 2
LANES = 128
ROPE_BASE = 10000.0
EPS = 1e-6

OFF_A_IN, OFF_A_GATE, OFF_Q, OFF_K, OFF_V, OFF_RGATE, OFF_GA, OFF_GB, D_IN = (
    0, 1024, 2048, 3072, 4096, 6144, 8192, 9216, 10240)

TILE = 256
TILES_PER_STEP = 2
STAGE_ROWS, STAGE_COLS = 256, 1024
STAGE_SLOTS = 8
VMEM_LIMIT_BYTES = 60 * 1024 * 1024

F32 = jnp.float32
BF16 = jnp.bfloat16


def _gammas():
    return [1.0 - 2.0 ** (-5.0 - h) for h in range(HEADS)]


def _silu(v):
    return v * jax.nn.sigmoid(v)


def _weight_blocks(pairs, rows, cols):
    blocks = []
    for w_hbm, w_vmem in pairs:
        k, n = w_hbm.shape
        assert k % rows == 0 and n % cols == 0
        blocks += [(w_hbm, w_vmem, r0, c0) for r0 in range(0, k, rows) for c0 in range(0, n, cols)]
    return blocks


def _load_cast_weights(rings):
    plans = []
    for pairs, stage, sem in rings:
        nslot, rows, cols = stage.shape
        blocks = _weight_blocks(pairs, rows, cols)

        def chunk_copy(i, blocks=blocks, stage=stage, sem=sem, nslot=nslot, rows=rows, cols=cols):
            w_hbm, _, r0, c0 = blocks[i]
            return pltpu.make_async_copy(w_hbm.at[r0:r0 + rows, c0:c0 + cols],
                                         stage.at[i % nslot], sem.at[i % nslot])

        for i in range(min(nslot, len(blocks))):
            chunk_copy(i).start(priority=i % 2)
        plans.append((blocks, stage, nslot, rows, cols, chunk_copy))
    for blocks, stage, nslot, rows, cols, chunk_copy in plans:
        for i, (_, w_vmem, r0, c0) in enumerate(blocks):
            chunk_copy(i).wait()
            w_vmem[r0:r0 + rows, c0:c0 + cols] = stage[i % nslot].astype(BF16)
            if i + nslot < len(blocks):
                chunk_copy(i + nslot).start(priority=i % 2)


def _block_kernel(x_ref, cosb_ref, sinb_ref, cosr_ref, sinr_ref,
                  nin_ref, nout_ref, pscale_ref, dmask_ref, xi_ref, zeta_ref,
                  w_in_hbm, w_pool_hbm, w_a_hbm, w_b_hbm, w_o_hbm,
                  o_ref,
                  w_in_s, w_pool_s, w_a_s, w_b_s, w_o_s, stage, stage_pool, sem, sem_pool,
                  ubuf, state, ybin):
    step = pl.program_id(1)

    @pl.when((pl.program_id(0) == 0) & (step == 0))
    def _load_weights():
        wide = ((w_in_hbm, w_in_s), (w_a_hbm, w_a_s), (w_b_hbm, w_b_s), (w_o_hbm, w_o_s))
        _load_cast_weights([(wide, stage, sem), (((w_pool_hbm, w_pool_s),), stage_pool, sem_pool)])

    @pl.when(step == 0)
    def _reset_carries():
        ubuf[0:POOL_HALO, :] = jnp.zeros((POOL_HALO, D_MODEL), F32)
        state[...] = jnp.zeros(state.shape, F32)

    for sub in range(TILES_PER_STEP):
        rows = slice(sub * TILE, (sub + 1) * TILE)
        _tile(step * TILES_PER_STEP + sub, x_ref.at[rows], o_ref.at[rows], cosb_ref.at[sub], sinb_ref.at[sub],
              cosr_ref, sinr_ref, nin_ref, nout_ref, pscale_ref, dmask_ref, xi_ref, zeta_ref,
              w_in_s, w_pool_s, w_a_s, w_b_s, w_o_s, ubuf, state, ybin)


def _tile(t, x_ref, o_ref, cosb_ref, sinb_ref, cosr_ref, sinr_ref, nin_ref, nout_ref, pscale_ref,
          dmask_ref, xi_ref, zeta_ref, w_in_s, w_pool_s, w_a_s, w_b_s, w_o_s, ubuf, state, ybin):
    x = x_ref[...]
    h = x * lax.rsqrt(jnp.mean(x * x, axis=-1, keepdims=True) + EPS) * nin_ref[...]
    hb = h.astype(BF16)

    def proj(lo, hi):
        return jnp.dot(hb, w_in_s[:, lo:hi], preferred_element_type=F32)

    cb, sb = cosb_ref[...], sinb_ref[...]
    cr, sr = cosr_ref[...], sinr_ref[...]
    cos = cb * cr - sb * sr
    sin = sb * cr + cb * sr

    def rotary(m, hd):
        m1 = m[:, hd * DK:hd * DK + HALF]
        m2 = m[:, hd * DK + HALF:(hd + 1) * DK]
        return jnp.concatenate([m1 * cos - m2 * sin, m1 * sin + m2 * cos], axis=1)

    ubuf[POOL_HALO:POOL_HALO + TILE, :] = proj(OFF_A_IN, OFF_A_GATE)
    q_all = proj(OFF_Q, OFF_K)
    qbs, qxs = [], []
    for hd in range(HEADS):
        qr = rotary(q_all, hd) * (DK ** -0.5)
        xi = xi_ref[hd]
        qbs.append(qr.astype(BF16))
        qxs.append((qr * jnp.concatenate([xi] * (DK // LANES), axis=1)).astype(BF16))
    k_all = proj(OFF_K, OFF_V)
    kbs, kzs = [], []
    for hd in range(HEADS):
        kr = rotary(k_all, hd)
        zeta = zeta_ref[hd]
        kbs.append(kr.astype(BF16))
        kzs.append((kr * jnp.concatenate([zeta] * (DK // LANES), axis=1)).astype(BF16))
    v_all = proj(OFF_V, OFF_RGATE)
    vbs = [v_all[:, hd * DV:(hd + 1) * DV].astype(BF16) for hd in range(HEADS)]

    pos1 = lax.broadcasted_iota(jnp.int32, (TILE, POOL_GROUP), 0) + (t * TILE + 1)
    mixed = []
    for g, w in enumerate(POOL_WINDOWS):
        cols = slice(g * POOL_GROUP, (g + 1) * POOL_GROUP)
        ext = ubuf[0:POOL_HALO + TILE, cols]
        win = ext
        span = 1
        while span < w:
            win = win + pltpu.roll(win, span, axis=0)
            span *= 2
        win = win[POOL_HALO:, :]
        u_g = ext[POOL_HALO:, :]
        cnt = jnp.minimum(pos1, w).astype(F32)
        pooled = win / cnt - u_g
        mixed.append(jnp.dot(pooled.astype(BF16), w_pool_s[g * POOL_GROUP:(g + 1) * POOL_GROUP, :],
                             preferred_element_type=F32))
    ubuf[0:POOL_HALO, :] = ubuf[TILE:TILE + POOL_HALO, :]
    a_gate = proj(OFF_A_GATE, OFF_Q)
    g_a = proj(OFF_GA, OFF_GB)
    ya_in = jnp.concatenate(mixed, axis=1) * pscale_ref[...] * _silu(a_gate)
    ya = jnp.dot(ya_in.astype(BF16), w_a_s[...], preferred_element_type=F32)
    merged = jax.nn.sigmoid(g_a) * ya

    gam = _gammas()
    ps, sts = [], []
    for hd in range(HEADS):
        scores = lax.dot_general(qbs[hd], kbs[hd], (((1,), (1,)), ((), ())),
                                 preferred_element_type=F32)
        st = state[hd]
        sts.append(st.astype(BF16))
        upd = lax.dot_general(kzs[hd], vbs[hd], (((0,), (0,)), ((), ())), preferred_element_type=F32)
        state[hd] = st * (gam[hd] ** TILE) + upd
        ps.append((scores * dmask_ref[hd]).astype(BF16))
    for hd in range(HEADS):
        lhs = jnp.concatenate([ps[hd], qxs[hd]], axis=1)
        rhs = jnp.concatenate([vbs[hd], sts[hd]], axis=0)
        o = jnp.dot(lhs, rhs, preferred_element_type=F32)
        on = o * lax.rsqrt(jnp.mean(o * o, axis=-1, keepdims=True) + EPS)
        rg = proj(OFF_RGATE + hd * DV, OFF_RGATE + (hd + 1) * DV)
        ybin[:, hd * DV:(hd + 1) * DV] = (on * _silu(rg)).astype(BF16)
    g_b = proj(OFF_GB, D_IN)
    yb = jnp.dot(ybin[...], w_b_s[...], preferred_element_type=F32)

    merged = merged + jax.nn.sigmoid(g_b) * yb
    y = x + jnp.dot(merged.astype(BF16), w_o_s[...], preferred_element_type=F32)
    o_ref[...] = y * lax.rsqrt(jnp.mean(y * y, axis=-1, keepdims=True) + EPS) * nout_ref[...]


def _decay_tables():
    log_g = np.log(np.asarray(_gammas(), np.float64))
    idx = np.arange(TILE)
    diff = idx[:, None] - idx[None, :]
    same = (idx[:, None] // CHUNK) == (idx[None, :] // CHUNK)
    dist = np.where(same, np.abs(diff), diff).astype(np.float64)
    keep = same | (diff > 0)
    dmask = np.where(keep[None], np.exp(log_g[:, None, None] * dist[None]), 0.0)
    xi = np.exp(log_g[:, None] * (idx[None, :] + 1.0))
    zeta = np.exp(log_g[:, None] * (TILE - 1.0 - idx[None, :]))
    rep = lambda a: np.ascontiguousarray(
        np.broadcast_to(a[:, :, None], (HEADS, TILE, LANES)), dtype=np.float32)
    return dmask.astype(np.float32), rep(xi), rep(zeta)


def _rope_tables(nt):
    inv_freq = 1.0 / (ROPE_BASE ** (jnp.arange(HALF, dtype=F32) / HALF))
    base = (jnp.arange(nt, dtype=jnp.int32) * TILE).astype(F32)[:, None] * inv_freq[None, :]
    rel = jnp.arange(TILE, dtype=jnp.int32).astype(F32)[:, None] * inv_freq[None, :]
    base = base.reshape(nt, 1, HALF)
    return jnp.cos(base), jnp.sin(base), jnp.cos(rel), jnp.sin(rel)


def _resident(shape):
    zeros = (0,) * len(shape)
    return pl.BlockSpec(shape, lambda b, t: zeros, pipeline_mode=pl.Buffered(1))


@jax.jit
def kernel(x, norm_in, w_in, w_pool, pool_scale, w_a, w_b, w_o, norm_out):
    batch, seq, d = x.shape
    step_rows = TILE * TILES_PER_STEP
    assert d == D_MODEL and seq % step_rows == 0 and TILE % CHUNK == 0
    nt = seq // TILE
    steps = seq // step_rows
    cosb, sinb, cosr, sinr = _rope_tables(nt)
    dmask, xi, zeta = _decay_tables()
    row = lambda v: v.reshape(1, -1).astype(F32)
    w_pool2d = w_pool.reshape(-1, POOL_GROUP)
    weights = (w_in, w_pool2d, w_a, w_b, w_o)
    tok_spec = pl.BlockSpec((step_rows, D_MODEL), lambda b, t: (b * steps + t, 0))
    base_spec = pl.BlockSpec((TILES_PER_STEP, 1, HALF), lambda b, t: (t, 0, 0))
    hbm_spec = pl.BlockSpec(memory_space=pl.ANY)
    out = pl.pallas_call(
        _block_kernel,
        grid=(batch, steps),
        in_specs=[
            tok_spec, base_spec, base_spec, _resident((TILE, HALF)), _resident((TILE, HALF)),
            _resident((1, D_MODEL)), _resident((1, D_MODEL)), _resident((1, D_MODEL)),
            _resident(dmask.shape), _resident(xi.shape), _resident(zeta.shape),
        ] + [hbm_spec] * len(weights),
        out_specs=tok_spec,
        out_shape=jax.ShapeDtypeStruct((batch * seq, D_MODEL), x.dtype),
        scratch_shapes=[pltpu.VMEM(w.shape, BF16) for w in weights] + [
            pltpu.VMEM((STAGE_SLOTS, STAGE_ROWS, STAGE_COLS), F32),
            pltpu.VMEM((w_pool2d.shape[0] // STAGE_ROWS, STAGE_ROWS, POOL_GROUP), F32),
            pltpu.SemaphoreType.DMA((STAGE_SLOTS,)),
            pltpu.SemaphoreType.DMA((w_pool2d.shape[0] // STAGE_ROWS,)),
            pltpu.VMEM((POOL_HALO + TILE, D_MODEL), F32),
            pltpu.VMEM((HEADS, DK, DV), F32),
            pltpu.VMEM((TILE, HEADS * DV), BF16),
        ],
        compiler_params=pltpu.CompilerParams(
            dimension_semantics=("arbitrary", "arbitrary"),
            vmem_limit_bytes=VMEM_LIMIT_BYTES),
        name="hybrid_block",
    )(x.reshape(batch * seq, D_MODEL), cosb, sinb, cosr, sinr,
      row(norm_in), row(norm_out), row(pool_scale), dmask, xi, zeta, *weights)
    return out.reshape(batch, seq, D_MODEL)
```

```python
import jax
import jax.numpy as jnp
import numpy as np
from jax import lax
from jax.experimental import pallas as pl
from jax.experimental.pallas import tpu as pltpu

D_MODEL = 1024
CHUNK = 64
POOL_WINDOWS = (2, 4, 8, 16)
POOL_GROUP = D_MODEL // len(POOL_WINDOWS)
POOL_HALO = max(POOL_WINDOWS)
HEADS = 4
DK = D_MODEL // HEADS
DV = 2 * D_MODEL // HEADS
HALF = DK // 2
LANES = 128
ROPE_BASE = 10000.0
EPS = 1e-6

OFF_A_IN, OFF_A_GATE, OFF_Q, OFF_K, OFF_V, OFF_RGATE, OFF_GA, OFF_GB, D_IN = (
    0, 1024, 2048, 3072, 4096, 6144, 8192, 9216, 10240)

TILE = 256
TILES_PER_STEP = 2
STAGE_ROWS, STAGE_COLS = 256, 1024
STAGE_SLOTS = 8
VMEM_LIMIT_BYTES = 60 * 1024 * 1024

F32 = jnp.float32
BF16 = jnp.bfloat16


def _gammas():
    return [1.0 - 2.0 ** (-5.0 - h) for h in range(HEADS)]


def _silu(v):
    return v * jax.nn.sigmoid(v)


def _weight_blocks(pairs, rows, cols):
    blocks = []
    for w_hbm, w_vmem in pairs:
        k, n = w_hbm.shape
        assert k % rows == 0 and n % cols == 0
        blocks += [(w_hbm, w_vmem, r0, c0) for r0 in range(0, k, rows) for c0 in range(0, n, cols)]
    return blocks


def _load_cast_weights(rings):
    plans = []
    for pairs, stage, sem in rings:
        nslot, rows, cols = stage.shape
        blocks = _weight_blocks(pairs, rows, cols)

        def chunk_copy(i, blocks=blocks, stage=stage, sem=sem, nslot=nslot, rows=rows, cols=cols):
            w_hbm, _, r0, c0 = blocks[i]
            return pltpu.make_async_copy(w_hbm.at[r0:r0 + rows, c0:c0 + cols],
                                         stage.at[i % nslot], sem.at[i % nslot])

        for i in range(min(nslot, len(blocks))):
            chunk_copy(i).start(priority=i % 2)
        plans.append((blocks, stage, nslot, rows, cols, chunk_copy))
    for blocks, stage, nslot, rows, cols, chunk_copy in plans:
        for i, (_, w_vmem, r0, c0) in enumerate(blocks):
            chunk_copy(i).wait()
            w_vmem[r0:r0 + rows, c0:c0 + cols] = stage[i % nslot].astype(BF16)
            if i + nslot < len(blocks):
                chunk_copy(i + nslot).start(priority=i % 2)


def _block_kernel(x_ref, cosb_ref, sinb_ref, cosr_ref, sinr_ref,
                  nin_ref, nout_ref, pscale_ref, dmask_ref, xi_ref, zeta_ref,
                  w_in_hbm, w_pool_hbm, w_a_hbm, w_b_hbm, w_o_hbm,
                  o_ref,
                  w_in_s, w_pool_s, w_a_s, w_b_s, w_o_s, stage, stage_pool, sem, sem_pool,
                  ubuf, state, ybin):
    step = pl.program_id(1)

    @pl.when((pl.program_id(0) == 0) & (step == 0))
    def _load_weights():
        wide = ((w_in_hbm, w_in_s), (w_a_hbm, w_a_s), (w_b_hbm, w_b_s), (w_o_hbm, w_o_s))
        _load_cast_weights([(wide, stage, sem), (((w_pool_hbm, w_pool_s),), stage_pool, sem_pool)])

    @pl.when(step == 0)
    def _reset_carries():
        ubuf[0:POOL_HALO, :] = jnp.zeros((POOL_HALO, D_MODEL), F32)
        state[...] = jnp.zeros(state.shape, F32)

    for sub in range(TILES_PER_STEP):
        rows = slice(sub * TILE, (sub + 1) * TILE)
        _tile(step * TILES_PER_STEP + sub, x_ref.at[rows], o_ref.at[rows], cosb_ref.at[sub], sinb_ref.at[sub],
              cosr_ref, sinr_ref, nin_ref, nout_ref, pscale_ref, dmask_ref, xi_ref, zeta_ref,
              w_in_s, w_pool_s, w_a_s, w_b_s, w_o_s, ubuf, state, ybin)


def _tile(t, x_ref, o_ref, cosb_ref, sinb_ref, cosr_ref, sinr_ref, nin_ref, nout_ref, pscale_ref,
          dmask_ref, xi_ref, zeta_ref, w_in_s, w_pool_s, w_a_s, w_b_s, w_o_s, ubuf, state, ybin):
    x = x_ref[...]
    h = x * lax.rsqrt(jnp.mean(x * x, axis=-1, keepdims=True) + EPS) * nin_ref[...]
    hb = h.astype(BF16)

    def proj(lo, hi):
        return jnp.dot(hb, w_in_s[:, lo:hi], preferred_element_type=F32)

    cb, sb = cosb_ref[...], sinb_ref[...]
    cr, sr = cosr_ref[...], sinr_ref[...]
    cos = cb * cr - sb * sr
    sin = sb * cr + cb * sr

    def rotary(m, hd):
        m1 = m[:, hd * DK:hd * DK + HALF]
        m2 = m[:, hd * DK + HALF:(hd + 1) * DK]
        return jnp.concatenate([m1 * cos - m2 * sin, m1 * sin + m2 * cos], axis=1)

    ubuf[POOL_HALO:POOL_HALO + TILE, :] = proj(OFF_A_IN, OFF_A_GATE)
    q_all = proj(OFF_Q, OFF_K)
    qbs, qxs = [], []
    for hd in range(HEADS):
        qr = rotary(q_all, hd) * (DK ** -0.5)
        xi = xi_ref[hd]
        qbs.append(qr.astype(BF16))
        qxs.append((qr * jnp.concatenate([xi] * (DK // LANES), axis=1)).astype(BF16))
    k_all = proj(OFF_K, OFF_V)
    kbs, kzs = [], []
    for hd in range(HEADS):
        kr = rotary(k_all, hd)
        zeta = zeta_ref[hd]
        kbs.append(kr.astype(BF16))
        kzs.append((kr * jnp.concatenate([zeta] * (DK // LANES), axis=1)).astype(BF16))
    v_all = proj(OFF_V, OFF_RGATE)
    vbs = [v_all[:, hd * DV:(hd + 1) * DV].astype(BF16) for hd in range(HEADS)]

    pos1 = lax.broadcasted_iota(jnp.int32, (TILE, POOL_GROUP), 0) + (t * TILE + 1)
    mixed = []
    for g, w in enumerate(POOL_WINDOWS):
        cols = slice(g * POOL_GROUP, (g + 1) * POOL_GROUP)
        ext = ubuf[0:POOL_HALO + TILE, cols]
        win = ext
        span = 1
        while span < w:
            win = win + pltpu.roll(win, span, axis=0)
            span *= 2
        win = win[POOL_HALO:, :]
        u_g = ext[POOL_HALO:, :]
        cnt = jnp.minimum(pos1, w).astype(F32)
        pooled = win / cnt - u_g
        mix = jnp.dot(pooled.astype(BF16), w_pool_s[g * POOL_GROUP:(g + 1) * POOL_GROUP, :],
                      preferred_element_type=F32)
        gate = proj(OFF_A_GATE + g * POOL_GROUP, OFF_A_GATE + (g + 1) * POOL_GROUP)
        mixed.append((mix * pscale_ref[:, cols] * _silu(gate)).astype(BF16))
    ubuf[0:POOL_HALO, :] = ubuf[TILE:TILE + POOL_HALO, :]
    g_a = proj(OFF_GA, OFF_GB)
    ya = jnp.dot(jnp.concatenate(mixed, axis=1), w_a_s[...], preferred_element_type=F32)
    merged = jax.nn.sigmoid(g_a) * ya

    gam = _gammas()
    ps, sts = [], []

    def state_side(hd):
        scores = lax.dot_general(qbs[hd], kbs[hd], (((1,), (1,)), ((), ())),
                                 preferred_element_type=F32)
        st = state[hd]
        sts.append(st.astype(BF16))
        upd = lax.dot_general(kzs[hd], vbs[hd], (((0,), (0,)), ((), ())), preferred_element_type=F32)
        state[hd] = st * (gam[hd] ** TILE) + upd
        ps.append((scores * dmask_ref[hd]).astype(BF16))

    def output_side(hd):
        lhs = jnp.concatenate([ps[hd], qxs[hd]], axis=1)
        rhs = jnp.concatenate([vbs[hd], sts[hd]], axis=0)
        o = jnp.dot(lhs, rhs, preferred_element_type=F32)
        on = o * lax.rsqrt(jnp.mean(o * o, axis=-1, keepdims=True) + EPS)
        rg = proj(OFF_RGATE + hd * DV, OFF_RGATE + (hd + 1) * DV)
        ybin[:, hd * DV:(hd + 1) * DV] = (on * _silu(rg)).astype(BF16)

    LAG = 2
    for hd in range(HEADS + LAG):
        if hd < HEADS:
            state_side(hd)
        if hd >= LAG:
            output_side(hd - LAG)
    g_b = proj(OFF_GB, D_IN)
    yb = jnp.dot(ybin[...], w_b_s[...], preferred_element_type=F32)

    merged = merged + jax.nn.sigmoid(g_b) * yb
    y = x + jnp.dot(merged.astype(BF16), w_o_s[...], preferred_element_type=F32)
    o_ref[...] = y * lax.rsqrt(jnp.mean(y * y, axis=-1, keepdims=True) + EPS) * nout_ref[...]


def _decay_tables():
    log_g = np.log(np.asarray(_gammas(), np.float64))
    idx = np.arange(TILE)
    diff = idx[:, None] - idx[None, :]
    same = (idx[:, None] // CHUNK) == (idx[None, :] // CHUNK)
    dist = np.where(same, np.abs(diff), diff).astype(np.float64)
    keep = same | (diff > 0)
    dmask = np.where(keep[None], np.exp(log_g[:, None, None] * dist[None]), 0.0)
    xi = np.exp(log_g[:, None] * (idx[None, :] + 1.0))
    zeta = np.exp(log_g[:, None] * (TILE - 1.0 - idx[None, :]))
    rep = lambda a: np.ascontiguousarray(
        np.broadcast_to(a[:, :, None], (HEADS, TILE, LANES)), dtype=np.float32)
    return dmask.astype(np.float32), rep(xi), rep(zeta)


def _rope_tables(nt):
    inv_freq = 1.0 / (ROPE_BASE ** (jnp.arange(HALF, dtype=F32) / HALF))
    base = (jnp.arange(nt, dtype=jnp.int32) * TILE).astype(F32)[:, None] * inv_freq[None, :]
    rel = jnp.arange(TILE, dtype=jnp.int32).astype(F32)[:, None] * inv_freq[None, :]
    base = base.reshape(nt, 1, HALF)
    return jnp.cos(base), jnp.sin(base), jnp.cos(rel), jnp.sin(rel)


def _resident(shape):
    zeros = (0,) * len(shape)
    return pl.BlockSpec(shape, lambda b, t: zeros, pipeline_mode=pl.Buffered(1))


@jax.jit
def kernel(x, norm_in, w_in, w_pool, pool_scale, w_a, w_b, w_o, norm_out):
    batch, seq, d = x.shape
    step_rows = TILE * TILES_PER_STEP
    assert d == D_MODEL and seq % step_rows == 0 and TILE % CHUNK == 0
    nt = seq // TILE
    steps = seq // step_rows
    cosb, sinb, cosr, sinr = _rope_tables(nt)
    dmask, xi, zeta = _decay_tables()
    row = lambda v: v.reshape(1, -1).astype(F32)
    w_pool2d = w_pool.reshape(-1, POOL_GROUP)
    weights = (w_in, w_pool2d, w_a, w_b, w_o)
    tok_spec = pl.BlockSpec((step_rows, D_MODEL), lambda b, t: (b * steps + t, 0))
    base_spec = pl.BlockSpec((TILES_PER_STEP, 1, HALF), lambda b, t: (t, 0, 0))
    hbm_spec = pl.BlockSpec(memory_space=pl.ANY)
    out = pl.pallas_call(
        _block_kernel,
        grid=(batch, steps),
        in_specs=[
            tok_spec, base_spec, base_spec, _resident((TILE, HALF)), _resident((TILE, HALF)),
            _resident((1, D_MODEL)), _resident((1, D_MODEL)), _resident((1, D_MODEL)),
            _resident(dmask.shape), _resident(xi.shape), _resident(zeta.shape),
        ] + [hbm_spec] * len(weights),
        out_specs=tok_spec,
        out_shape=jax.ShapeDtypeStruct((batch * seq, D_MODEL), x.dtype),
        scratch_shapes=[pltpu.VMEM(w.shape, BF16) for w in weights] + [
            pltpu.VMEM((STAGE_SLOTS, STAGE_ROWS, STAGE_COLS), F32),
            pltpu.VMEM((w_pool2d.shape[0] // STAGE_ROWS, STAGE_ROWS, POOL_GROUP), F32),
            pltpu.SemaphoreType.DMA((STAGE_SLOTS,)),
            pltpu.SemaphoreType.DMA((w_pool2d.shape[0] // STAGE_ROWS,)),
            pltpu.VMEM((POOL_HALO + TILE, D_MODEL), F32),
            pltpu.VMEM((HEADS, DK, DV), F32),
            pltpu.VMEM((TILE, HEADS * DV), BF16),
        ],
        compiler_params=pltpu.CompilerParams(
            dimension_semantics=("arbitrary", "arbitrary"),
            vmem_limit_bytes=VMEM_LIMIT_BYTES),
        name="hybrid_block",
    )(x.reshape(batch * seq, D_MODEL), cosb, sinb, cosr, sinr,
      row(norm_in), row(norm_out), row(pool_scale), dmask, xi, zeta, *weights)
    return out.reshape(batch, seq, D_MODEL)
```

```python
import jax
import jax.numpy as jnp
import numpy as np
from jax import lax
from jax.experimental import pallas as pl
from jax.experimental.pallas import tpu as pltpu

D_MODEL = 1024
CHUNK = 64
POOL_WINDOWS = (2, 4, 8, 16)
POOL_GROUP = D_MODEL // len(POOL_WINDOWS)
POOL_HALO = max(POOL_WINDOWS)
HEADS = 4
DK = D_MODEL // HEADS
DV = 2 * D_MODEL // HEADS
HALF = DK // 2
LANES = 128
ROPE_BASE = 10000.0
EPS = 1e-6

OFF_A_IN, OFF_A_GATE, OFF_Q, OFF_K, OFF_V, OFF_RGATE, OFF_GA, OFF_GB, D_IN = (
    0, 1024, 2048, 3072, 4096, 6144, 8192, 9216, 10240)

TILE = 256
TILES_PER_STEP = 2
STAGE_ROWS, STAGE_COLS = 256, 1024
STAGE_SLOTS = 8
VMEM_LIMIT_BYTES = 60 * 1024 * 1024

F32 = jnp.float32
BF16 = jnp.bfloat16


def _gammas():
    return [1.0 - 2.0 ** (-5.0 - h) for h in range(HEADS)]


def _silu(v):
    return v * jax.nn.sigmoid(v)


class _WeightStream:
    def __init__(self, parts, stage, sem):
        self.stage, self.sem = stage, sem
        self.nslot, rows, cols = stage.shape
        self.blocks, self.end = [], {}
        for name, w_hbm, w_vmem in parts:
            k, n = w_hbm.shape
            assert k % rows == 0 and n % cols == 0
            for r0 in range(0, k, rows):
                for c0 in range(0, n, cols):
                    self.blocks.append((w_hbm.at[r0:r0 + rows, c0:c0 + cols], w_vmem.at[r0:r0 + rows, c0:c0 + cols]))
            self.end[name] = len(self.blocks)
        self.done = 0

    def _copy(self, i):
        slot = i % self.nslot
        return pltpu.make_async_copy(self.blocks[i][0], self.stage.at[slot], self.sem.at[slot])

    def start(self):
        for i in range(min(self.nslot, len(self.blocks))):
            self._copy(i).start(priority=i % 2)

    def need(self, name=None):
        upto = len(self.blocks) if name is None else self.end[name]
        while self.done < upto:
            i = self.done
            self._copy(i).wait()
            self.blocks[i][1][...] = self.stage[i % self.nslot].astype(BF16)
            if i + self.nslot < len(self.blocks):
                self._copy(i + self.nslot).start(priority=i % 2)
            self.done += 1


W_IN_USE_ORDER = (0, 2, 3, 4, 5, 1, 8, "w_a", 6, 7, 9, "w_b", "w_o")


def _block_kernel(x_ref, cosb_ref, sinb_ref, cosr_ref, sinr_ref,
                  nin_ref, nout_ref, pscale_ref, dmask_ref, xi_ref, zeta_ref,
                  w_in_hbm, w_pool_hbm, w_a_hbm, w_b_hbm, w_o_hbm,
                  o_ref,
                  w_in_s, w_pool_s, w_a_s, w_b_s, w_o_s, stage, stage_pool, sem, sem_pool,
                  ubuf, state, ybin):
    step = pl.program_id(1)
    first_step = (pl.program_id(0) == 0) & (step == 0)

    @pl.when(step == 0)
    def _reset_carries():
        ubuf[0:POOL_HALO, :] = jnp.zeros((POOL_HALO, D_MODEL), F32)
        state[...] = jnp.zeros(state.shape, F32)

    def tile(sub, weights_ready=None):
        rows = slice(sub * TILE, (sub + 1) * TILE)
        _tile(step * TILES_PER_STEP + sub, x_ref.at[rows], o_ref.at[rows], cosb_ref.at[sub], sinb_ref.at[sub],
              cosr_ref, sinr_ref, nin_ref, nout_ref, pscale_ref, dmask_ref, xi_ref, zeta_ref,
              w_in_s, w_pool_s, w_a_s, w_b_s, w_o_s, ubuf, state, ybin, weights_ready)

    @pl.when(jnp.logical_not(first_step))
    def _steady_state():
        for sub in range(TILES_PER_STEP):
            tile(sub)

    @pl.when(first_step)
    def _first_step():
        others = {"w_a": (w_a_hbm, w_a_s), "w_b": (w_b_hbm, w_b_s), "w_o": (w_o_hbm, w_o_s)}
        parts = []
        for key in W_IN_USE_ORDER:
            if isinstance(key, int):
                cols = slice(key * STAGE_COLS, (key + 1) * STAGE_COLS)
                parts.append((key, w_in_hbm.at[:, cols], w_in_s.at[:, cols]))
            else:
                parts.append((key,) + others[key])
        wide = _WeightStream(parts, stage, sem)
        pool = _WeightStream([("w_pool", w_pool_hbm, w_pool_s)], stage_pool, sem_pool)
        wide.start()
        pool.start()

        def weights_ready(name, lo=0, hi=0):
            if name == "w_pool":
                pool.need()
            elif name == "w_in":
                for blk in range(lo // STAGE_COLS, (hi - 1) // STAGE_COLS + 1):
                    wide.need(blk)
            else:
                wide.need(name)

        tile(0, weights_ready)
        wide.need()
        pool.need()
        for sub in range(1, TILES_PER_STEP):
            tile(sub)


def _tile(t, x_ref, o_ref, cosb_ref, sinb_ref, cosr_ref, sinr_ref, nin_ref, nout_ref, pscale_ref,
          dmask_ref, xi_ref, zeta_ref, w_in_s, w_pool_s, w_a_s, w_b_s, w_o_s, ubuf, state, ybin,
          weights_ready=None):
    ready = weights_ready if weights_ready is not None else (lambda *a: None)
    x = x_ref[...]
    h = x * lax.rsqrt(jnp.mean(x * x, axis=-1, keepdims=True) + EPS) * nin_ref[...]
    hb = h.astype(BF16)

    def proj(lo, hi):
        ready("w_in", lo, hi)
        return jnp.dot(hb, w_in_s[:, lo:hi], preferred_element_type=F32)

    cb, sb = cosb_ref[...], sinb_ref[...]
    cr, sr = cosr_ref[...], sinr_ref[...]
    cos = cb * cr - sb * sr
    sin = sb * cr + cb * sr

    def rotary(m, hd):
        m1 = m[:, hd * DK:hd * DK + HALF]
        m2 = m[:, hd * DK + HALF:(hd + 1) * DK]
        return jnp.concatenate([m1 * cos - m2 * sin, m1 * sin + m2 * cos], axis=1)

    ubuf[POOL_HALO:POOL_HALO + TILE, :] = proj(OFF_A_IN, OFF_A_GATE)
    q_all = proj(OFF_Q, OFF_K)
    qbs, qxs = [], []
    for hd in range(HEADS):
        qr = rotary(q_all, hd) * (DK ** -0.5)
        xi = xi_ref[hd]
        qbs.append(qr.astype(BF16))
        qxs.append((qr * jnp.concatenate([xi] * (DK // LANES), axis=1)).astype(BF16))
    k_all = proj(OFF_K, OFF_V)
    kbs, kzs = [], []
    for hd in range(HEADS):
        kr = rotary(k_all, hd)
        zeta = zeta_ref[hd]
        kbs.append(kr.astype(BF16))
        kzs.append((kr * jnp.concatenate([zeta] * (DK // LANES), axis=1)).astype(BF16))
    v_all = proj(OFF_V, OFF_RGATE)
    vbs = [v_all[:, hd * DV:(hd + 1) * DV].astype(BF16) for hd in range(HEADS)]

    pos1 = lax.broadcasted_iota(jnp.int32, (TILE, POOL_GROUP), 0) + (t * TILE + 1)
    mixed = []
    for g, w in enumerate(POOL_WINDOWS):
        cols = slice(g * POOL_GROUP, (g + 1) * POOL_GROUP)
        ext = ubuf[0:POOL_HALO + TILE, cols]
        win = ext
        span = 1
        while span < w:
            win = win + pltpu.roll(win, span, axis=0)
            span *= 2
        win = win[POOL_HALO:, :]
        u_g = ext[POOL_HALO:, :]
        cnt = jnp.minimum(pos1, w).astype(F32)
        pooled = win / cnt - u_g
        ready("w_pool")
        mix = jnp.dot(pooled.astype(BF16), w_pool_s[g * POOL_GROUP:(g + 1) * POOL_GROUP, :],
                      preferred_element_type=F32)
        gate = proj(OFF_A_GATE + g * POOL_GROUP, OFF_A_GATE + (g + 1) * POOL_GROUP)
        mixed.append((mix * pscale_ref[:, cols] * _silu(gate)).astype(BF16))
    ubuf[0:POOL_HALO, :] = ubuf[TILE:TILE + POOL_HALO, :]
    g_a = proj(OFF_GA, OFF_GB)
    ready("w_a")
    ya = jnp.dot(jnp.concatenate(mixed, axis=1), w_a_s[...], preferred_element_type=F32)
    merged = jax.nn.sigmoid(g_a) * ya

    gam = _gammas()
    ps, sts = [], []

    def state_side(hd):
        scores = lax.dot_general(qbs[hd], kbs[hd], (((1,), (1,)), ((), ())),
                                 preferred_element_type=F32)
        st = state[hd]
        sts.append(st.astype(BF16))
        upd = lax.dot_general(kzs[hd], vbs[hd], (((0,), (0,)), ((), ())), preferred_element_type=F32)
        state[hd] = st * (gam[hd] ** TILE) + upd
        ps.append((scores * dmask_ref[hd]).astype(BF16))

    def output_side(hd):
        lhs = jnp.concatenate([ps[hd], qxs[hd]], axis=1)
        rhs = jnp.concatenate([vbs[hd], sts[hd]], axis=0)
        o = jnp.dot(lhs, rhs, preferred_element_type=F32)
        on = o * lax.rsqrt(jnp.mean(o * o, axis=-1, keepdims=True) + EPS)
        rg = proj(OFF_RGATE + hd * DV, OFF_RGATE + (hd + 1) * DV)
        ybin[:, hd * DV:(hd + 1) * DV] = (on * _silu(rg)).astype(BF16)

    LAG = 2
    for hd in range(HEADS + LAG):
        if hd < HEADS:
            state_side(hd)
        if hd >= LAG:
            output_side(hd - LAG)
    g_b = proj(OFF_GB, D_IN)
    ready("w_b")
    yb = jnp.dot(ybin[...], w_b_s[...], preferred_element_type=F32)

    merged = merged + jax.nn.sigmoid(g_b) * yb
    ready("w_o")
    y = x + jnp.dot(merged.astype(BF16), w_o_s[...], preferred_element_type=F32)
    o_ref[...] = y * lax.rsqrt(jnp.mean(y * y, axis=-1, keepdims=True) + EPS) * nout_ref[...]


def _decay_tables():
    log_g = np.log(np.asarray(_gammas(), np.float64))
    idx = np.arange(TILE)
    diff = idx[:, None] - idx[None, :]
    same = (idx[:, None] // CHUNK) == (idx[None, :] // CHUNK)
    dist = np.where(same, np.abs(diff), diff).astype(np.float64)
    keep = same | (diff > 0)
    dmask = np.where(keep[None], np.exp(log_g[:, None, None] * dist[None]), 0.0)
    xi = np.exp(log_g[:, None] * (idx[None, :] + 1.0))
    zeta = np.exp(log_g[:, None] * (TILE - 1.0 - idx[None, :]))
    rep = lambda a: np.ascontiguousarray(
        np.broadcast_to(a[:, :, None], (HEADS, TILE, LANES)), dtype=np.float32)
    return dmask.astype(np.float32), rep(xi), rep(zeta)


def _rope_tables(nt):
    inv_freq = 1.0 / (ROPE_BASE ** (jnp.arange(HALF, dtype=F32) / HALF))
    base = (jnp.arange(nt, dtype=jnp.int32) * TILE).astype(F32)[:, None] * inv_freq[None, :]
    rel = jnp.arange(TILE, dtype=jnp.int32).astype(F32)[:, None] * inv_freq[None, :]
    base = base.reshape(nt, 1, HALF)
    return jnp.cos(base), jnp.sin(base), jnp.cos(rel), jnp.sin(rel)


def _resident(shape):
    zeros = (0,) * len(shape)
    return pl.BlockSpec(shape, lambda b, t: zeros, pipeline_mode=pl.Buffered(1))


@jax.jit
def kernel(x, norm_in, w_in, w_pool, pool_scale, w_a, w_b, w_o, norm_out):
    batch, seq, d = x.shape
    step_rows = TILE * TILES_PER_STEP
    assert d == D_MODEL and seq % step_rows == 0 and TILE % CHUNK == 0
    nt = seq // TILE
    steps = seq // step_rows
    cosb, sinb, cosr, sinr = _rope_tables(nt)
    dmask, xi, zeta = _decay_tables()
    row = lambda v: v.reshape(1, -1).astype(F32)
    w_pool2d = w_pool.reshape(-1, POOL_GROUP)
    weights = (w_in, w_pool2d, w_a, w_b, w_o)
    tok_spec = pl.BlockSpec((step_rows, D_MODEL), lambda b, t: (b * steps + t, 0))
    base_spec = pl.BlockSpec((TILES_PER_STEP, 1, HALF), lambda b, t: (t, 0, 0))
    hbm_spec = pl.BlockSpec(memory_space=pl.ANY)
    out = pl.pallas_call(
        _block_kernel,
        grid=(batch, steps),
        in_specs=[
            tok_spec, base_spec, base_spec, _resident((TILE, HALF)), _resident((TILE, HALF)),
            _resident((1, D_MODEL)), _resident((1, D_MODEL)), _resident((1, D_MODEL)),
            _resident(dmask.shape), _resident(xi.shape), _resident(zeta.shape),
        ] + [hbm_spec] * len(weights),
        out_specs=tok_spec,
        out_shape=jax.ShapeDtypeStruct((batch * seq, D_MODEL), x.dtype),
        scratch_shapes=[pltpu.VMEM(w.shape, BF16) for w in weights] + [
            pltpu.VMEM((STAGE_SLOTS, STAGE_ROWS, STAGE_COLS), F32),
            pltpu.VMEM((w_pool2d.shape[0] // STAGE_ROWS, STAGE_ROWS, POOL_GROUP), F32),
            pltpu.SemaphoreType.DMA((STAGE_SLOTS,)),
            pltpu.SemaphoreType.DMA((w_pool2d.shape[0] // STAGE_ROWS,)),
            pltpu.VMEM((POOL_HALO + TILE, D_MODEL), F32),
            pltpu.VMEM((HEADS, DK, DV), F32),
            pltpu.VMEM((TILE, HEADS * DV), BF16),
        ],
        compiler_params=pltpu.CompilerParams(
            dimension_semantics=("arbitrary", "arbitrary"),
            vmem_limit_bytes=VMEM_LIMIT_BYTES),
        name="hybrid_block",
    )(x.reshape(batch * seq, D_MODEL), cosb, sinb, cosr, sinr,
      row(norm_in), row(norm_out), row(pool_scale), dmask, xi, zeta, *weights)
    return out.reshape(batch, seq, D_MODEL)
```

```python
import jax
import jax.numpy as jnp
import numpy as np
from jax import lax
from jax.experimental import pallas as pl
from jax.experimental.pallas import tpu as pltpu

D_MODEL = 1024
CHUNK = 64
POOL_WINDOWS = (2, 4, 8, 16)
POOL_GROUP = D_MODEL // len(POOL_WINDOWS)
POOL_HALO = max(POOL_WINDOWS)
HEADS = 4
DK = D_MODEL // HEADS
DV = 2 * D_MODEL // HEADS
HALF = DK // 2
LANES = 128
ROPE_BASE = 10000.0
EPS = 1e-6

OFF_A_IN, OFF_A_GATE, OFF_Q, OFF_K, OFF_V, OFF_RGATE, OFF_GA, OFF_GB, D_IN = (
    0, 1024, 2048, 3072, 4096, 6144, 8192, 9216, 10240)

TILE = 256
TILES_PER_STEP = 2
STAGE_ROWS, STAGE_COLS = 256, 1024
STAGE_SLOTS = 2
STAGE_IN_ROWS = 32
STAGE_IN_SLOTS = 8
VMEM_LIMIT_BYTES = 63 * 1024 * 1024

F32 = jnp.float32
BF16 = jnp.bfloat16


def _gammas():
    return [1.0 - 2.0 ** (-5.0 - h) for h in range(HEADS)]


def _silu(v):
    return v * jax.nn.sigmoid(v)


def _weight_blocks(pairs, rows, cols):
    blocks = []
    for w_hbm, w_vmem in pairs:
        k, n = w_hbm.shape
        assert k % rows == 0 and n % cols == 0
        blocks += [(w_hbm, w_vmem, r0, c0) for r0 in range(0, k, rows) for c0 in range(0, n, cols)]
    return blocks


def _load_cast_weights(rings):
    plans = []
    for pairs, stage, sem in rings:
        nslot, rows, cols = stage.shape
        blocks = _weight_blocks(pairs, rows, cols)

        def chunk_copy(i, blocks=blocks, stage=stage, sem=sem, nslot=nslot, rows=rows, cols=cols):
            w_hbm, _, r0, c0 = blocks[i]
            return pltpu.make_async_copy(w_hbm.at[r0:r0 + rows, c0:c0 + cols],
                                         stage.at[i % nslot], sem.at[i % nslot])

        for i in range(min(nslot, len(blocks))):
            chunk_copy(i).start(priority=i % 2)
        plans.append((blocks, stage, nslot, rows, cols, chunk_copy))
    done = [0] * len(plans)
    while any(d < len(p[0]) for d, p in zip(done, plans)):
        r = min((r for r in range(len(plans)) if done[r] < len(plans[r][0])),
                key=lambda r: done[r] / len(plans[r][0]))
        blocks, stage, nslot, rows, cols, chunk_copy = plans[r]
        i = done[r]
        _, w_vmem, r0, c0 = blocks[i]
        chunk_copy(i).wait()
        w_vmem[r0:r0 + rows, c0:c0 + cols] = stage[i % nslot].astype(BF16)
        if i + nslot < len(blocks):
            chunk_copy(i + nslot).start(priority=i % 2)
        done[r] += 1


def _block_kernel(x_ref, cosb_ref, sinb_ref, cosr_ref, sinr_ref,
                  nin_ref, nout_ref, pscale_ref, dmask_ref, xi_ref, zeta_ref,
                  w_in_hbm, w_pool_hbm, w_a_hbm, w_b_hbm, w_o_hbm,
                  o_ref,
                  w_in_s, w_pool_s, w_a_s, w_b_s, w_o_s,
                  stage_in, stage, stage_pool, sem_in, sem, sem_pool,
                  ubuf, state, ybin):
    step = pl.program_id(1)

    @pl.when((pl.program_id(0) == 0) & (step == 0))
    def _load_weights():
        square = ((w_a_hbm, w_a_s), (w_b_hbm, w_b_s), (w_o_hbm, w_o_s))
        _load_cast_weights([(((w_in_hbm, w_in_s),), stage_in, sem_in), (square, stage, sem),
                            (((w_pool_hbm, w_pool_s),), stage_pool, sem_pool)])

    @pl.when(step == 0)
    def _reset_carries():
        ubuf[0:POOL_HALO, :] = jnp.zeros((POOL_HALO, D_MODEL), F32)
        state[...] = jnp.zeros(state.shape, F32)

    for sub in range(TILES_PER_STEP):
        rows = slice(sub * TILE, (sub + 1) * TILE)
        _tile(step * TILES_PER_STEP + sub, x_ref.at[rows], o_ref.at[rows], cosb_ref.at[sub], sinb_ref.at[sub],
              cosr_ref, sinr_ref, nin_ref, nout_ref, pscale_ref, dmask_ref, xi_ref, zeta_ref,
              w_in_s, w_pool_s, w_a_s, w_b_s, w_o_s, ubuf, state, ybin)


def _tile(t, x_ref, o_ref, cosb_ref, sinb_ref, cosr_ref, sinr_ref, nin_ref, nout_ref, pscale_ref,
          dmask_ref, xi_ref, zeta_ref, w_in_s, w_pool_s, w_a_s, w_b_s, w_o_s, ubuf, state, ybin):
    x = x_ref[...]
    h = x * lax.rsqrt(jnp.mean(x * x, axis=-1, keepdims=True) + EPS) * nin_ref[...]
    hb = h.astype(BF16)

    def proj(lo, hi):
        return jnp.dot(hb, w_in_s[:, lo:hi], preferred_element_type=F32)

    cb, sb = cosb_ref[...], sinb_ref[...]
    cr, sr = cosr_ref[...], sinr_ref[...]
    cos = cb * cr - sb * sr
    sin = sb * cr + cb * sr

    def rotary(m, hd):
        m1 = m[:, hd * DK:hd * DK + HALF]
        m2 = m[:, hd * DK + HALF:(hd + 1) * DK]
        return jnp.concatenate([m1 * cos - m2 * sin, m1 * sin + m2 * cos], axis=1)

    ubuf[POOL_HALO:POOL_HALO + TILE, :] = proj(OFF_A_IN, OFF_A_GATE)
    q_all = proj(OFF_Q, OFF_K)
    qbs, qxs = [], []
    for hd in range(HEADS):
        qr = rotary(q_all, hd) * (DK ** -0.5)
        xi = xi_ref[hd]
        qbs.append(qr.astype(BF16))
        qxs.append((qr * jnp.concatenate([xi] * (DK // LANES), axis=1)).astype(BF16))
    k_all = proj(OFF_K, OFF_V)
    kbs, kzs = [], []
    for hd in range(HEADS):
        kr = rotary(k_all, hd)
        zeta = zeta_ref[hd]
        kbs.append(kr.astype(BF16))
        kzs.append((kr * jnp.concatenate([zeta] * (DK // LANES), axis=1)).astype(BF16))
    v_all = proj(OFF_V, OFF_RGATE)
    vbs = [v_all[:, hd * DV:(hd + 1) * DV].astype(BF16) for hd in range(HEADS)]

    pos1 = lax.broadcasted_iota(jnp.int32, (TILE, POOL_GROUP), 0) + (t * TILE + 1)
    mixed = []
    for g, w in enumerate(POOL_WINDOWS):
        cols = slice(g * POOL_GROUP, (g + 1) * POOL_GROUP)
        ext = ubuf[0:POOL_HALO + TILE, cols]
        win = ext
        span = 1
        while span < w:
            win = win + pltpu.roll(win, span, axis=0)
            span *= 2
        win = win[POOL_HALO:, :]
        u_g = ext[POOL_HALO:, :]
        cnt = jnp.minimum(pos1, w).astype(F32)
        pooled = win / cnt - u_g
        mix = jnp.dot(pooled.astype(BF16), w_pool_s[g * POOL_GROUP:(g + 1) * POOL_GROUP, :],
                      preferred_element_type=F32)
        gate = proj(OFF_A_GATE + g * POOL_GROUP, OFF_A_GATE + (g + 1) * POOL_GROUP)
        mixed.append((mix * pscale_ref[:, cols] * _silu(gate)).astype(BF16))
    ubuf[0:POOL_HALO, :] = ubuf[TILE:TILE + POOL_HALO, :]
    g_a = proj(OFF_GA, OFF_GB)
    ya = jnp.dot(jnp.concatenate(mixed, axis=1), w_a_s[...], preferred_element_type=F32)
    merged = jax.nn.sigmoid(g_a) * ya

    gam = _gammas()
    ps, sts = [], []

    def state_side(hd):
        scores = lax.dot_general(qbs[hd], kbs[hd], (((1,), (1,)), ((), ())),
                                 preferred_element_type=F32)
        st = state[hd]
        sts.append(st.astype(BF16))
        upd = lax.dot_general(kzs[hd], vbs[hd], (((0,), (0,)), ((), ())), preferred_element_type=F32)
        state[hd] = st * (gam[hd] ** TILE) + upd
        ps.append((scores * dmask_ref[hd]).astype(BF16))

    def output_side(hd):
        lhs = jnp.concatenate([ps[hd], qxs[hd]], axis=1)
        rhs = jnp.concatenate([vbs[hd], sts[hd]], axis=0)
        o = jnp.dot(lhs, rhs, preferred_element_type=F32)
        on = o * lax.rsqrt(jnp.mean(o * o, axis=-1, keepdims=True) + EPS)
        rg = proj(OFF_RGATE + hd * DV, OFF_RGATE + (hd + 1) * DV)
        ybin[:, hd * DV:(hd + 1) * DV] = (on * _silu(rg)).astype(BF16)

    LAG = 2
    for hd in range(HEADS + LAG):
        if hd < HEADS:
            state_side(hd)
        if hd >= LAG:
            output_side(hd - LAG)
    g_b = proj(OFF_GB, D_IN)
    yb = jnp.dot(ybin[...], w_b_s[...], preferred_element_type=F32)

    merged = merged + jax.nn.sigmoid(g_b) * yb
    y = x + jnp.dot(merged.astype(BF16), w_o_s[...], preferred_element_type=F32)
    o_ref[...] = y * lax.rsqrt(jnp.mean(y * y, axis=-1, keepdims=True) + EPS) * nout_ref[...]


def _decay_tables():
    log_g = np.log(np.asarray(_gammas(), np.float64))
    idx = np.arange(TILE)
    diff = idx[:, None] - idx[None, :]
    same = (idx[:, None] // CHUNK) == (idx[None, :] // CHUNK)
    dist = np.where(same, np.abs(diff), diff).astype(np.float64)
    keep = same | (diff > 0)
    dmask = np.where(keep[None], np.exp(log_g[:, None, None] * dist[None]), 0.0)
    xi = np.exp(log_g[:, None] * (idx[None, :] + 1.0))
    zeta = np.exp(log_g[:, None] * (TILE - 1.0 - idx[None, :]))
    rep = lambda a: np.ascontiguousarray(
        np.broadcast_to(a[:, :, None], (HEADS, TILE, LANES)), dtype=np.float32)
    return dmask.astype(np.float32), rep(xi), rep(zeta)


def _rope_tables(nt):
    inv_freq = 1.0 / (ROPE_BASE ** (jnp.arange(HALF, dtype=F32) / HALF))
    base = (jnp.arange(nt, dtype=jnp.int32) * TILE).astype(F32)[:, None] * inv_freq[None, :]
    rel = jnp.arange(TILE, dtype=jnp.int32).astype(F32)[:, None] * inv_freq[None, :]
    base = base.reshape(nt, 1, HALF)
    return jnp.cos(base), jnp.sin(base), jnp.cos(rel), jnp.sin(rel)


def _resident(shape):
    zeros = (0,) * len(shape)
    return pl.BlockSpec(shape, lambda b, t: zeros, pipeline_mode=pl.Buffered(1))


@jax.jit
def kernel(x, norm_in, w_in, w_pool, pool_scale, w_a, w_b, w_o, norm_out):
    batch, seq, d = x.shape
    step_rows = TILE * TILES_PER_STEP
    assert d == D_MODEL and seq % step_rows == 0 and TILE % CHUNK == 0
    nt = seq // TILE
    steps = seq // step_rows
    cosb, sinb, cosr, sinr = _rope_tables(nt)
    dmask, xi, zeta = _decay_tables()
    row = lambda v: v.reshape(1, -1).astype(F32)
    w_pool2d = w_pool.reshape(-1, POOL_GROUP)
    weights = (w_in, w_pool2d, w_a, w_b, w_o)
    tok_spec = pl.BlockSpec((step_rows, D_MODEL), lambda b, t: (b * steps + t, 0))
    base_spec = pl.BlockSpec((TILES_PER_STEP, 1, HALF), lambda b, t: (t, 0, 0))
    hbm_spec = pl.BlockSpec(memory_space=pl.ANY)
    out = pl.pallas_call(
        _block_kernel,
        grid=(batch, steps),
        in_specs=[
            tok_spec, base_spec, base_spec, _resident((TILE, HALF)), _resident((TILE, HALF)),
            _resident((1, D_MODEL)), _resident((1, D_MODEL)), _resident((1, D_MODEL)),
            _resident(dmask.shape), _resident(xi.shape), _resident(zeta.shape),
        ] + [hbm_spec] * len(weights),
        out_specs=tok_spec,
        out_shape=jax.ShapeDtypeStruct((batch * seq, D_MODEL), x.dtype),
        scratch_shapes=[pltpu.VMEM(w.shape, BF16) for w in weights] + [
            pltpu.VMEM((STAGE_IN_SLOTS, STAGE_IN_ROWS, D_IN), F32),
            pltpu.VMEM((STAGE_SLOTS, STAGE_ROWS, STAGE_COLS), F32),
            pltpu.VMEM((w_pool2d.shape[0] // STAGE_ROWS, STAGE_ROWS, POOL_GROUP), F32),
            pltpu.SemaphoreType.DMA((STAGE_IN_SLOTS,)),
            pltpu.SemaphoreType.DMA((STAGE_SLOTS,)),
            pltpu.SemaphoreType.DMA((w_pool2d.shape[0] // STAGE_ROWS,)),
            pltpu.VMEM((POOL_HALO + TILE, D_MODEL), F32),
            pltpu.VMEM((HEADS, DK, DV), F32),
            pltpu.VMEM((TILE, HEADS * DV), BF16),
        ],
        compiler_params=pltpu.CompilerParams(
            dimension_semantics=("arbitrary", "arbitrary"),
            vmem_limit_bytes=VMEM_LIMIT_BYTES),
        name="hybrid_block",
    )(x.reshape(batch * seq, D_MODEL), cosb, sinb, cosr, sinr,
      row(norm_in), row(norm_out), row(pool_scale), dmask, xi, zeta, *weights)
    return out.reshape(batch, seq, D_MODEL)
```

```python
import jax
import jax.numpy as jnp
import numpy as np
from jax import lax
from jax.experimental import pallas as pl
from jax.experimental.pallas import tpu as pltpu

D_MODEL = 1024
CHUNK = 64
POOL_WINDOWS = (2, 4, 8, 16)
POOL_GROUP = D_MODEL // len(POOL_WINDOWS)
POOL_HALO = max(POOL_WINDOWS)
HEADS = 4
DK = D_MODEL // HEADS
DV = 2 * D_MODEL // HEADS
HALF = DK // 2
LANES = 128
ROPE_BASE = 10000.0
EPS = 1e-6

OFF_A_IN, OFF_A_GATE, OFF_Q, OFF_K, OFF_V, OFF_RGATE, OFF_GA, OFF_GB, D_IN = (
    0, 1024, 2048, 3072, 4096, 6144, 8192, 9216, 10240)

TILE = 256
TILES_PER_STEP = 2
STEP_ROWS = TILE * TILES_PER_STEP
STAGE_ROWS, STAGE_COLS = 256, 1024
STAGE_SLOTS = 8
VMEM_LIMIT_BYTES = 60 * 1024 * 1024

F32 = jnp.float32
BF16 = jnp.bfloat16


def _gammas():
    return [1.0 - 2.0 ** (-5.0 - h) for h in range(HEADS)]


def _silu(v):
    return v * jax.nn.sigmoid(v)


def _weight_blocks(pairs, rows, cols):
    blocks = []
    for w_hbm, w_vmem in pairs:
        k, n = w_hbm.shape
        assert k % rows == 0 and n % cols == 0
        blocks += [(w_hbm, w_vmem, r0, c0) for r0 in range(0, k, rows) for c0 in range(0, n, cols)]
    return blocks


def _load_cast_weights(rings):
    plans = []
    for pairs, stage, sem in rings:
        nslot, rows, cols = stage.shape
        blocks = _weight_blocks(pairs, rows, cols)

        def chunk_copy(i, blocks=blocks, stage=stage, sem=sem, nslot=nslot, rows=rows, cols=cols):
            w_hbm, _, r0, c0 = blocks[i]
            return pltpu.make_async_copy(w_hbm.at[r0:r0 + rows, c0:c0 + cols],
                                         stage.at[i % nslot], sem.at[i % nslot])

        for i in range(min(nslot, len(blocks))):
            chunk_copy(i).start(priority=i % 2)
        plans.append((blocks, stage, nslot, rows, cols, chunk_copy))
    for blocks, stage, nslot, rows, cols, chunk_copy in plans:
        for i, (_, w_vmem, r0, c0) in enumerate(blocks):
            chunk_copy(i).wait()
            w_vmem[r0:r0 + rows, c0:c0 + cols] = stage[i % nslot].astype(BF16)
            if i + nslot < len(blocks):
                chunk_copy(i + nslot).start(priority=i % 2)


def _block_kernel(x_hbm, cosb_ref, sinb_ref, cosr_ref, sinr_ref,
                  nin_ref, nout_ref, pscale_ref, dmask_ref, xi_ref, zeta_ref,
                  w_in_hbm, w_pool_hbm, w_a_hbm, w_b_hbm, w_o_hbm,
                  o_hbm,
                  w_in_s, w_pool_s, w_a_s, w_b_s, w_o_s, stage, stage_pool, sem, sem_pool,
                  ubuf, state, ybin, xbuf, obuf, xsem, osem):
    total = x_hbm.shape[0] // STEP_ROWS
    steps = cosb_ref.shape[0] // TILES_PER_STEP

    def x_copy(s, slot):
        r0 = pl.multiple_of(s * STEP_ROWS, STEP_ROWS)
        return pltpu.make_async_copy(x_hbm.at[pl.ds(r0, STEP_ROWS)], xbuf.at[slot], xsem.at[slot])

    def o_copy(s, slot):
        r0 = pl.multiple_of(s * STEP_ROWS, STEP_ROWS)
        return pltpu.make_async_copy(obuf.at[slot], o_hbm.at[pl.ds(r0, STEP_ROWS)], osem.at[slot])

    x_copy(0, 0).start()
    wide = ((w_in_hbm, w_in_s), (w_a_hbm, w_a_s), (w_b_hbm, w_b_s), (w_o_hbm, w_o_s))
    _load_cast_weights([(wide, stage, sem), (((w_pool_hbm, w_pool_s),), stage_pool, sem_pool)])

    def _step(s, carry):
        slot = s % 2
        step = s % steps

        @pl.when(s + 1 < total)
        def _prefetch():
            x_copy(s + 1, 1 - slot).start()

        x_copy(s, slot).wait()

        @pl.when(s >= 2)
        def _drain():
            o_copy(s - 2, slot).wait()

        @pl.when(step == 0)
        def _reset_carries():
            ubuf[0:POOL_HALO, :] = jnp.zeros((POOL_HALO, D_MODEL), F32)
            state[...] = jnp.zeros(state.shape, F32)

        for sub in range(TILES_PER_STEP):
            rows = slice(sub * TILE, (sub + 1) * TILE)
            t = step * TILES_PER_STEP + sub
            _tile(t, xbuf.at[slot, rows], obuf.at[slot, rows], cosb_ref.at[t], sinb_ref.at[t],
                  cosr_ref, sinr_ref, nin_ref, nout_ref, pscale_ref, dmask_ref, xi_ref, zeta_ref,
                  w_in_s, w_pool_s, w_a_s, w_b_s, w_o_s, ubuf, state, ybin)

        o_copy(s, slot).start()
        return carry

    lax.fori_loop(0, total, _step, 0)
    o_copy(total - 2, total % 2).wait()
    o_copy(total - 1, (total - 1) % 2).wait()


def _tile(t, x_ref, o_ref, cosb_ref, sinb_ref, cosr_ref, sinr_ref, nin_ref, nout_ref, pscale_ref,
          dmask_ref, xi_ref, zeta_ref, w_in_s, w_pool_s, w_a_s, w_b_s, w_o_s, ubuf, state, ybin):
    x = x_ref[...]
    h = x * lax.rsqrt(jnp.mean(x * x, axis=-1, keepdims=True) + EPS) * nin_ref[...]
    hb = h.astype(BF16)

    def proj(lo, hi):
        return jnp.dot(hb, w_in_s[:, lo:hi], preferred_element_type=F32)

    cb, sb = cosb_ref[...], sinb_ref[...]
    cr, sr = cosr_ref[...], sinr_ref[...]
    cos = cb * cr - sb * sr
    sin = sb * cr + cb * sr

    def rotary(m, hd):
        m1 = m[:, hd * DK:hd * DK + HALF]
        m2 = m[:, hd * DK + HALF:(hd + 1) * DK]
        return jnp.concatenate([m1 * cos - m2 * sin, m1 * sin + m2 * cos], axis=1)

    ubuf[POOL_HALO:POOL_HALO + TILE, :] = proj(OFF_A_IN, OFF_A_GATE)
    q_all = proj(OFF_Q, OFF_K)
    qbs, qxs = [], []
    for hd in range(HEADS):
        qr = rotary(q_all, hd) * (DK ** -0.5)
        xi = xi_ref[hd]
        qbs.append(qr.astype(BF16))
        qxs.append((qr * jnp.concatenate([xi] * (DK // LANES), axis=1)).astype(BF16))
    k_all = proj(OFF_K, OFF_V)
    kbs, kzs = [], []
    for hd in range(HEADS):
        kr = rotary(k_all, hd)
        zeta = zeta_ref[hd]
        kbs.append(kr.astype(BF16))
        kzs.append((kr * jnp.concatenate([zeta] * (DK // LANES), axis=1)).astype(BF16))
    v_all = proj(OFF_V, OFF_RGATE)
    vbs = [v_all[:, hd * DV:(hd + 1) * DV].astype(BF16) for hd in range(HEADS)]

    pos1 = lax.broadcasted_iota(jnp.int32, (TILE, POOL_GROUP), 0) + (t * TILE + 1)
    mixed = []
    for g, w in enumerate(POOL_WINDOWS):
        cols = slice(g * POOL_GROUP, (g + 1) * POOL_GROUP)
        ext = ubuf[0:POOL_HALO + TILE, cols]
        win = ext
        span = 1
        while span < w:
            win = win + pltpu.roll(win, span, axis=0)
            span *= 2
        win = win[POOL_HALO:, :]
        u_g = ext[POOL_HALO:, :]
        cnt = jnp.minimum(pos1, w).astype(F32)
        pooled = win / cnt - u_g
        mix = jnp.dot(pooled.astype(BF16), w_pool_s[g * POOL_GROUP:(g + 1) * POOL_GROUP, :],
                      preferred_element_type=F32)
        gate = proj(OFF_A_GATE + g * POOL_GROUP, OFF_A_GATE + (g + 1) * POOL_GROUP)
        mixed.append((mix * pscale_ref[:, cols] * _silu(gate)).astype(BF16))
    ubuf[0:POOL_HALO, :] = ubuf[TILE:TILE + POOL_HALO, :]
    g_a = proj(OFF_GA, OFF_GB)
    ya = jnp.dot(jnp.concatenate(mixed, axis=1), w_a_s[...], preferred_element_type=F32)
    merged = jax.nn.sigmoid(g_a) * ya

    gam = _gammas()
    ps, sts = [], []

    def state_side(hd):
        scores = lax.dot_general(qbs[hd], kbs[hd], (((1,), (1,)), ((), ())),
                                 preferred_element_type=F32)
        st = state[hd]
        sts.append(st.astype(BF16))
        upd = lax.dot_general(kzs[hd], vbs[hd], (((0,), (0,)), ((), ())), preferred_element_type=F32)
        state[hd] = st * (gam[hd] ** TILE) + upd
        ps.append((scores * dmask_ref[hd]).astype(BF16))

    def output_side(hd):
        lhs = jnp.concatenate([ps[hd], qxs[hd]], axis=1)
        rhs = jnp.concatenate([vbs[hd], sts[hd]], axis=0)
        o = jnp.dot(lhs, rhs, preferred_element_type=F32)
        on = o * lax.rsqrt(jnp.mean(o * o, axis=-1, keepdims=True) + EPS)
        rg = proj(OFF_RGATE + hd * DV, OFF_RGATE + (hd + 1) * DV)
        ybin[:, hd * DV:(hd + 1) * DV] = (on * _silu(rg)).astype(BF16)

    LAG = 2
    for hd in range(HEADS + LAG):
        if hd < HEADS:
            state_side(hd)
        if hd >= LAG:
            output_side(hd - LAG)
    g_b = proj(OFF_GB, D_IN)
    yb = jnp.dot(ybin[...], w_b_s[...], preferred_element_type=F32)

    merged = merged + jax.nn.sigmoid(g_b) * yb
    y = x + jnp.dot(merged.astype(BF16), w_o_s[...], preferred_element_type=F32)
    o_ref[...] = y * lax.rsqrt(jnp.mean(y * y, axis=-1, keepdims=True) + EPS) * nout_ref[...]


def _decay_tables():
    log_g = np.log(np.asarray(_gammas(), np.float64))
    idx = np.arange(TILE)
    diff = idx[:, None] - idx[None, :]
    same = (idx[:, None] // CHUNK) == (idx[None, :] // CHUNK)
    dist = np.where(same, np.abs(diff), diff).astype(np.float64)
    keep = same | (diff > 0)
    dmask = np.where(keep[None], np.exp(log_g[:, None, None] * dist[None]), 0.0)
    xi = np.exp(log_g[:, None] * (idx[None, :] + 1.0))
    zeta = np.exp(log_g[:, None] * (TILE - 1.0 - idx[None, :]))
    rep = lambda a: np.ascontiguousarray(
        np.broadcast_to(a[:, :, None], (HEADS, TILE, LANES)), dtype=np.float32)
    return dmask.astype(np.float32), rep(xi), rep(zeta)


def _rope_tables(nt):
    inv_freq = 1.0 / (ROPE_BASE ** (jnp.arange(HALF, dtype=F32) / HALF))
    base = (jnp.arange(nt, dtype=jnp.int32) * TILE).astype(F32)[:, None] * inv_freq[None, :]
    rel = jnp.arange(TILE, dtype=jnp.int32).astype(F32)[:, None] * inv_freq[None, :]
    base = base.reshape(nt, 1, HALF)
    return jnp.cos(base), jnp.sin(base), jnp.cos(rel), jnp.sin(rel)


def _resident(shape):
    zeros = (0,) * len(shape)
    return pl.BlockSpec(shape, lambda i: zeros, pipeline_mode=pl.Buffered(1))


@jax.jit
def kernel(x, norm_in, w_in, w_pool, pool_scale, w_a, w_b, w_o, norm_out):
    batch, seq, d = x.shape
    assert d == D_MODEL and seq % STEP_ROWS == 0 and TILE % CHUNK == 0
    assert batch * seq // STEP_ROWS >= 2
    nt = seq // TILE
    cosb, sinb, cosr, sinr = _rope_tables(nt)
    dmask, xi, zeta = _decay_tables()
    row = lambda v: v.reshape(1, -1).astype(F32)
    w_pool2d = w_pool.reshape(-1, POOL_GROUP)
    weights = (w_in, w_pool2d, w_a, w_b, w_o)
    hbm_spec = pl.BlockSpec(memory_space=pl.ANY)
    out = pl.pallas_call(
        _block_kernel,
        grid=(1,),
        in_specs=[
            hbm_spec, _resident(cosb.shape), _resident(sinb.shape), _resident((TILE, HALF)), _resident((TILE, HALF)),
            _resident((1, D_MODEL)), _resident((1, D_MODEL)), _resident((1, D_MODEL)),
            _resident(dmask.shape), _resident(xi.shape), _resident(zeta.shape),
        ] + [hbm_spec] * len(weights),
        out_specs=hbm_spec,
        out_shape=jax.ShapeDtypeStruct((batch * seq, D_MODEL), x.dtype),
        scratch_shapes=[pltpu.VMEM(w.shape, BF16) for w in weights] + [
            pltpu.VMEM((STAGE_SLOTS, STAGE_ROWS, STAGE_COLS), F32),
            pltpu.VMEM((w_pool2d.shape[0] // STAGE_ROWS, STAGE_ROWS, POOL_GROUP), F32),
            pltpu.SemaphoreType.DMA((STAGE_SLOTS,)),
            pltpu.SemaphoreType.DMA((w_pool2d.shape[0] // STAGE_ROWS,)),
            pltpu.VMEM((POOL_HALO + TILE, D_MODEL), F32),
            pltpu.VMEM((HEADS, DK, DV), F32),
            pltpu.VMEM((TILE, HEADS * DV), BF16),
            pltpu.VMEM((2, STEP_ROWS, D_MODEL), F32),
            pltpu.VMEM((2, STEP_ROWS, D_MODEL), F32),
            pltpu.SemaphoreType.DMA((2,)),
            pltpu.SemaphoreType.DMA((2,)),
        ],
        compiler_params=pltpu.CompilerParams(
            dimension_semantics=("arbitrary",),
            vmem_limit_bytes=VMEM_LIMIT_BYTES),
        name="hybrid_block",
    )(x.reshape(batch * seq, D_MODEL), cosb, sinb, cosr, sinr,
      row(norm_in), row(norm_out), row(pool_scale), dmask, xi, zeta, *weights)
    return out.reshape(batch, seq, D_MODEL)
```

```python
import jax
import jax.numpy as jnp
import numpy as np
from jax import lax
from jax.experimental import pallas as pl
from jax.experimental.pallas import tpu as pltpu

D_MODEL = 1024
CHUNK = 64
POOL_WINDOWS = (2, 4, 8, 16)
POOL_GROUP = D_MODEL // len(POOL_WINDOWS)
POOL_HALO = max(POOL_WINDOWS)
HEADS = 4
DK = D_MODEL // HEADS
DV = 2 * D_MODEL // HEADS
HALF = DK // 2
LANES = 128
ROPE_BASE = 10000.0
EPS = 1e-6

OFF_A_IN, OFF_A_GATE, OFF_Q, OFF_K, OFF_V, OFF_RGATE, OFF_GA, OFF_GB, D_IN = (
    0, 1024, 2048, 3072, 4096, 6144, 8192, 9216, 10240)

TILE = 256
TILES_PER_STEP = 2
STEP_ROWS = TILE * TILES_PER_STEP
STAGE_ROWS, STAGE_COLS = 256, 1024
STAGE_SLOTS = 8
VMEM_LIMIT_BYTES = 63 * 1024 * 1024

F32 = jnp.float32
BF16 = jnp.bfloat16


def _gammas():
    return [1.0 - 2.0 ** (-5.0 - h) for h in range(HEADS)]


def _silu(v):
    return v * jax.nn.sigmoid(v)


def _weight_blocks(pairs, rows, cols):
    blocks = []
    for w_hbm, w_vmem in pairs:
        k, n = w_hbm.shape
        assert k % rows == 0 and n % cols == 0
        blocks += [(w_hbm, w_vmem, r0, c0) for r0 in range(0, k, rows) for c0 in range(0, n, cols)]
    return blocks


def _load_cast_weights(rings):
    plans = []
    for pairs, stage, sem in rings:
        nslot, rows, cols = stage.shape
        blocks = _weight_blocks(pairs, rows, cols)

        def chunk_copy(i, blocks=blocks, stage=stage, sem=sem, nslot=nslot, rows=rows, cols=cols):
            w_hbm, _, r0, c0 = blocks[i]
            return pltpu.make_async_copy(w_hbm.at[r0:r0 + rows, c0:c0 + cols],
                                         stage.at[i % nslot], sem.at[i % nslot])

        for i in range(min(nslot, len(blocks))):
            chunk_copy(i).start(priority=i % 2)
        plans.append((blocks, stage, nslot, rows, cols, chunk_copy))
    for blocks, stage, nslot, rows, cols, chunk_copy in plans:
        for i, (_, w_vmem, r0, c0) in enumerate(blocks):
            chunk_copy(i).wait()
            w_vmem[r0:r0 + rows, c0:c0 + cols] = stage[i % nslot].astype(BF16)
            if i + nslot < len(blocks):
                chunk_copy(i + nslot).start(priority=i % 2)


def _block_kernel(x_hbm, cosb_ref, sinb_ref, cosr_ref, sinr_ref,
                  nin_ref, nout_ref, pscale_ref, dmask_ref, xi_ref, zeta_ref,
                  w_in_hbm, w_pool_hbm, w_a_hbm, w_b_hbm, w_o_hbm,
                  o_hbm,
                  w_in_s, w_pool_s, w_a_s, w_b_s, w_o_s, stage, stage_pool, sem, sem_pool,
                  ubuf, state, ybin, xbuf, obuf, xsem, osem, hbn, qn):
    total = x_hbm.shape[0] // STEP_ROWS
    steps = cosb_ref.shape[0] // TILES_PER_STEP

    def x_copy(s, slot):
        r0 = pl.multiple_of(s * STEP_ROWS, STEP_ROWS)
        return pltpu.make_async_copy(x_hbm.at[pl.ds(r0, STEP_ROWS)], xbuf.at[slot], xsem.at[slot])

    def o_copy(s, slot):
        r0 = pl.multiple_of(s * STEP_ROWS, STEP_ROWS)
        return pltpu.make_async_copy(obuf.at[slot], o_hbm.at[pl.ds(r0, STEP_ROWS)], osem.at[slot])

    x_copy(0, 0).start()
    x_copy(1, 1).start()
    wide = ((w_in_hbm, w_in_s), (w_a_hbm, w_a_s), (w_b_hbm, w_b_s), (w_o_hbm, w_o_s))
    _load_cast_weights([(wide, stage, sem), (((w_pool_hbm, w_pool_s),), stage_pool, sem_pool)])
    x_copy(0, 0).wait()
    _tile_head(xbuf.at[0, 0:TILE], nin_ref, w_in_s, hbn, qn)

    def _step(s, carry):
        xslot = s % 3
        xnext = (s + 1) % 3
        slot = s % 2
        step = s % steps

        @pl.when(s + 2 < total)
        def _prefetch():
            x_copy(s + 2, (s + 2) % 3).start()

        @pl.when(s + 1 < total)
        def _arrive():
            x_copy(s + 1, xnext).wait()

        @pl.when(s >= 2)
        def _drain():
            o_copy(s - 2, slot).wait()

        @pl.when(step == 0)
        def _reset_carries():
            ubuf[0:POOL_HALO, :] = jnp.zeros((POOL_HALO, D_MODEL), F32)
            state[...] = jnp.zeros(state.shape, F32)

        for sub in range(TILES_PER_STEP):
            rows = slice(sub * TILE, (sub + 1) * TILE)
            t = step * TILES_PER_STEP + sub
            _tile(t, xbuf.at[xslot, rows], obuf.at[slot, rows], cosb_ref.at[t], sinb_ref.at[t],
                  cosr_ref, sinr_ref, nin_ref, nout_ref, pscale_ref, dmask_ref, xi_ref, zeta_ref,
                  w_in_s, w_pool_s, w_a_s, w_b_s, w_o_s, ubuf, state, ybin,
                  pre=(hbn, qn) if sub == 0 else None)

        _tile_head(xbuf.at[xnext, 0:TILE], nin_ref, w_in_s, hbn, qn)
        o_copy(s, slot).start()
        return carry

    lax.fori_loop(0, total, _step, 0)
    o_copy(total - 2, total % 2).wait()
    o_copy(total - 1, (total - 1) % 2).wait()


def _norm_in(x, nin_ref):
    h = x * lax.rsqrt(jnp.mean(x * x, axis=-1, keepdims=True) + EPS) * nin_ref[...]
    return h.astype(BF16)


def _tile_head(x_ref, nin_ref, w_in_s, hb_ref, q_ref):
    hb = _norm_in(x_ref[...], nin_ref)
    hb_ref[...] = hb
    q_ref[...] = jnp.dot(hb, w_in_s[:, OFF_Q:OFF_K], preferred_element_type=F32)


def _tile(t, x_ref, o_ref, cosb_ref, sinb_ref, cosr_ref, sinr_ref, nin_ref, nout_ref, pscale_ref,
          dmask_ref, xi_ref, zeta_ref, w_in_s, w_pool_s, w_a_s, w_b_s, w_o_s, ubuf, state, ybin, pre=None):
    x = x_ref[...]
    hb = _norm_in(x, nin_ref) if pre is None else pre[0][...]

    def proj(lo, hi):
        return jnp.dot(hb, w_in_s[:, lo:hi], preferred_element_type=F32)

    cb, sb = cosb_ref[...], sinb_ref[...]
    cr, sr = cosr_ref[...], sinr_ref[...]
    cos = cb * cr - sb * sr
    sin = sb * cr + cb * sr

    def rotary(m, hd):
        m1 = m[:, hd * DK:hd * DK + HALF]
        m2 = m[:, hd * DK + HALF:(hd + 1) * DK]
        return jnp.concatenate([m1 * cos - m2 * sin, m1 * sin + m2 * cos], axis=1)

    ubuf[POOL_HALO:POOL_HALO + TILE, :] = proj(OFF_A_IN, OFF_A_GATE)
    q_all = proj(OFF_Q, OFF_K) if pre is None else pre[1][...]
    qbs, qxs = [], []
    for hd in range(HEADS):
        qr = rotary(q_all, hd) * (DK ** -0.5)
        xi = xi_ref[hd]
        qbs.append(qr.astype(BF16))
        qxs.append((qr * jnp.concatenate([xi] * (DK // LANES), axis=1)).astype(BF16))
    k_all = proj(OFF_K, OFF_V)
    kbs, kzs = [], []
    for hd in range(HEADS):
        kr = rotary(k_all, hd)
        zeta = zeta_ref[hd]
        kbs.append(kr.astype(BF16))
        kzs.append((kr * jnp.concatenate([zeta] * (DK // LANES), axis=1)).astype(BF16))
    v_all = proj(OFF_V, OFF_RGATE)
    vbs = [v_all[:, hd * DV:(hd + 1) * DV].astype(BF16) for hd in range(HEADS)]

    pos1 = lax.broadcasted_iota(jnp.int32, (TILE, POOL_GROUP), 0) + (t * TILE + 1)
    mixed = []
    for g, w in enumerate(POOL_WINDOWS):
        cols = slice(g * POOL_GROUP, (g + 1) * POOL_GROUP)
        ext = ubuf[0:POOL_HALO + TILE, cols]
        win = ext
        span = 1
        while span < w:
            win = win + pltpu.roll(win, span, axis=0)
            span *= 2
        win = win[POOL_HALO:, :]
        u_g = ext[POOL_HALO:, :]
        cnt = jnp.minimum(pos1, w).astype(F32)
        pooled = win / cnt - u_g
        mix = jnp.dot(pooled.astype(BF16), w_pool_s[g * POOL_GROUP:(g + 1) * POOL_GROUP, :],
                      preferred_element_type=F32)
        gate = proj(OFF_A_GATE + g * POOL_GROUP, OFF_A_GATE + (g + 1) * POOL_GROUP)
        mixed.append((mix * pscale_ref[:, cols] * _silu(gate)).astype(BF16))
    ubuf[0:POOL_HALO, :] = ubuf[TILE:TILE + POOL_HALO, :]
    g_a = proj(OFF_GA, OFF_GB)
    ya = jnp.dot(jnp.concatenate(mixed, axis=1), w_a_s[...], preferred_element_type=F32)
    merged = jax.nn.sigmoid(g_a) * ya

    gam = _gammas()
    ps, sts = [], []

    def state_side(hd):
        scores = lax.dot_general(qbs[hd], kbs[hd], (((1,), (1,)), ((), ())),
                                 preferred_element_type=F32)
        st = state[hd]
        sts.append(st.astype(BF16))
        upd = lax.dot_general(kzs[hd], vbs[hd], (((0,), (0,)), ((), ())), preferred_element_type=F32)
        state[hd] = st * (gam[hd] ** TILE) + upd
        ps.append((scores * dmask_ref[hd]).astype(BF16))

    def output_side(hd):
        lhs = jnp.concatenate([ps[hd], qxs[hd]], axis=1)
        rhs = jnp.concatenate([vbs[hd], sts[hd]], axis=0)
        o = jnp.dot(lhs, rhs, preferred_element_type=F32)
        on = o * lax.rsqrt(jnp.mean(o * o, axis=-1, keepdims=True) + EPS)
        rg = proj(OFF_RGATE + hd * DV, OFF_RGATE + (hd + 1) * DV)
        ybin[:, hd * DV:(hd + 1) * DV] = (on * _silu(rg)).astype(BF16)

    LAG = 2
    for hd in range(HEADS + LAG):
        if hd < HEADS:
            state_side(hd)
        if hd >= LAG:
            output_side(hd - LAG)
    g_b = proj(OFF_GB, D_IN)
    yb = jnp.dot(ybin[...], w_b_s[...], preferred_element_type=F32)

    merged = merged + jax.nn.sigmoid(g_b) * yb
    y = x + jnp.dot(merged.astype(BF16), w_o_s[...], preferred_element_type=F32)
    o_ref[...] = y * lax.rsqrt(jnp.mean(y * y, axis=-1, keepdims=True) + EPS) * nout_ref[...]


def _decay_tables():
    log_g = np.log(np.asarray(_gammas(), np.float64))
    idx = np.arange(TILE)
    diff = idx[:, None] - idx[None, :]
    same = (idx[:, None] // CHUNK) == (idx[None, :] // CHUNK)
    dist = np.where(same, np.abs(diff), diff).astype(np.float64)
    keep = same | (diff > 0)
    dmask = np.where(keep[None], np.exp(log_g[:, None, None] * dist[None]), 0.0)
    xi = np.exp(log_g[:, None] * (idx[None, :] + 1.0))
    zeta = np.exp(log_g[:, None] * (TILE - 1.0 - idx[None, :]))
    rep = lambda a: np.ascontiguousarray(
        np.broadcast_to(a[:, :, None], (HEADS, TILE, LANES)), dtype=np.float32)
    return dmask.astype(np.float32), rep(xi), rep(zeta)


def _rope_tables(nt):
    inv_freq = 1.0 / (ROPE_BASE ** (jnp.arange(HALF, dtype=F32) / HALF))
    base = (jnp.arange(nt, dtype=jnp.int32) * TILE).astype(F32)[:, None] * inv_freq[None, :]
    rel = jnp.arange(TILE, dtype=jnp.int32).astype(F32)[:, None] * inv_freq[None, :]
    base = base.reshape(nt, 1, HALF)
    return jnp.cos(base), jnp.sin(base), jnp.cos(rel), jnp.sin(rel)


def _resident(shape):
    zeros = (0,) * len(shape)
    return pl.BlockSpec(shape, lambda i: zeros, pipeline_mode=pl.Buffered(1))


@jax.jit
def kernel(x, norm_in, w_in, w_pool, pool_scale, w_a, w_b, w_o, norm_out):
    batch, seq, d = x.shape
    assert d == D_MODEL and seq % STEP_ROWS == 0 and TILE % CHUNK == 0
    assert batch * seq // STEP_ROWS >= 2
    nt = seq // TILE
    cosb, sinb, cosr, sinr = _rope_tables(nt)
    dmask, xi, zeta = _decay_tables()
    row = lambda v: v.reshape(1, -1).astype(F32)
    w_pool2d = w_pool.reshape(-1, POOL_GROUP)
    weights = (w_in, w_pool2d, w_a, w_b, w_o)
    hbm_spec = pl.BlockSpec(memory_space=pl.ANY)
    out = pl.pallas_call(
        _block_kernel,
        grid=(1,),
        in_specs=[
            hbm_spec, _resident(cosb.shape), _resident(sinb.shape), _resident((TILE, HALF)), _resident((TILE, HALF)),
            _resident((1, D_MODEL)), _resident((1, D_MODEL)), _resident((1, D_MODEL)),
            _resident(dmask.shape), _resident(xi.shape), _resident(zeta.shape),
        ] + [hbm_spec] * len(weights),
        out_specs=hbm_spec,
        out_shape=jax.ShapeDtypeStruct((batch * seq, D_MODEL), x.dtype),
        scratch_shapes=[pltpu.VMEM(w.shape, BF16) for w in weights] + [
            pltpu.VMEM((STAGE_SLOTS, STAGE_ROWS, STAGE_COLS), F32),
            pltpu.VMEM((w_pool2d.shape[0] // STAGE_ROWS, STAGE_ROWS, POOL_GROUP), F32),
            pltpu.SemaphoreType.DMA((STAGE_SLOTS,)),
            pltpu.SemaphoreType.DMA((w_pool2d.shape[0] // STAGE_ROWS,)),
            pltpu.VMEM((POOL_HALO + TILE, D_MODEL), F32),
            pltpu.VMEM((HEADS, DK, DV), F32),
            pltpu.VMEM((TILE, HEADS * DV), BF16),
            pltpu.VMEM((3, STEP_ROWS, D_MODEL), F32),
            pltpu.VMEM((2, STEP_ROWS, D_MODEL), F32),
            pltpu.SemaphoreType.DMA((3,)),
            pltpu.SemaphoreType.DMA((2,)),
            pltpu.VMEM((TILE, D_MODEL), BF16),
            pltpu.VMEM((TILE, OFF_K - OFF_Q), F32),
        ],
        compiler_params=pltpu.CompilerParams(
            dimension_semantics=("arbitrary",),
            vmem_limit_bytes=VMEM_LIMIT_BYTES),
        name="hybrid_block",
    )(x.reshape(batch * seq, D_MODEL), cosb, sinb, cosr, sinr,
      row(norm_in), row(norm_out), row(pool_scale), dmask, xi, zeta, *weights)
    return out.reshape(batch, seq, D_MODEL)
```
